```python
import jax, jax.numpy as jnp
from jax import lax
import numpy as np

D_MODEL = 2048
BATCH = 16
SEQ = 2048
DEPTH = 4
DEC_BATCH = 4
DEC_SEQ = 4096
PAST_LEN = 128

N_META = 16
N_MIXERS = 2
N_RET = (DEPTH + N_MIXERS - 1) // N_MIXERS
N_CONV = DEPTH // N_MIXERS
RET_HEADS = 8
RET_DK = D_MODEL // RET_HEADS
RET_DV = 2 * D_MODEL // RET_HEADS
CHUNK = 128
ROPE_BASE = 10000.0
CONV_K = 31
FFN_HIDDEN = ((8 * D_MODEL // 3 + 127) // 128) * 128
FFN_CONV_K = 3
EPS = 1e-6

kernel_name = "bidir_retention_conformer_hybrid_encoder"


def rms_norm(x, g):
    xf = x.astype(jnp.float32)
    y = xf * lax.rsqrt(jnp.mean(xf * xf, axis=-1, keepdims=True) + EPS)
    return (y * g.astype(jnp.float32)).astype(x.dtype)


def layer_norm(x, g, b):
    xf = x.astype(jnp.float32)
    mu = jnp.mean(xf, axis=-1, keepdims=True)
    var = jnp.mean(jnp.square(xf - mu), axis=-1, keepdims=True)
    y = (xf - mu) * lax.rsqrt(var + EPS)
    return (y * g.astype(jnp.float32) + b.astype(jnp.float32)).astype(x.dtype)


def depthwise_conv(x, w, b):
    k = w.shape[0]
    y = lax.conv_general_dilated(
        x, w[:, None, :].astype(x.dtype), window_strides=(1,),
        padding=[((k - 1) // 2, (k - 1) // 2)],
        dimension_numbers=('NWC', 'WIO', 'NWC'),
        feature_group_count=x.shape[-1])
    return y + b.astype(x.dtype)


def rotary(x, pos):
    half = x.shape[-1] // 2
    inv = ROPE_BASE ** (-jnp.arange(half, dtype=jnp.float32) / half)
    ang = pos.astype(jnp.float32)[:, None] * inv[None, :]
    cos = jnp.cos(ang)[None, :, None, :]
    sin = jnp.sin(ang)[None, :, None, :]
    xf = x.astype(jnp.float32)
    x1, x2 = xf[..., :half], xf[..., half:]
    return jnp.concatenate([x1 * cos - x2 * sin, x2 * cos + x1 * sin], axis=-1).astype(x.dtype)


def retention(h, w_in, decay_f, decay_b, w_out):
    b, L, _ = h.shape
    hk, hv = RET_HEADS * RET_DK, RET_HEADS * RET_DV
    proj = h @ w_in
    qkv, g = proj[..., :2 * hk + hv], proj[..., 2 * hk + hv:]
    pad = (-L) % CHUNK
    qkv = jnp.pad(qkv, ((0, 0), (pad, 0), (0, 0)))
    lp = L + pad
    nc = lp // CHUNK
    pos = jnp.arange(lp)
    q = rotary(qkv[..., :hk].reshape(b, lp, RET_HEADS, RET_DK), pos)
    k = rotary(qkv[..., hk:2 * hk].reshape(b, lp, RET_HEADS, RET_DK), pos) * (RET_DK ** -0.5)
    v = qkv[..., 2 * hk:].reshape(b, lp, RET_HEADS, RET_DV)
    q = q.reshape(b, nc, CHUNK, RET_HEADS, RET_DK)
    k = k.reshape(b, nc, CHUNK, RET_HEADS, RET_DK)
    v = v.reshape(b, nc, CHUNK, RET_HEADS, RET_DV)
    dt = q.dtype

    lgf = jax.nn.log_sigmoid(decay_f.astype(jnp.float32))
    lgb = jax.nn.log_sigmoid(decay_b.astype(jnp.float32))
    idx = jnp.arange(CHUNK, dtype=jnp.float32)
    dist = idx[:, None] - idx[None, :]
    adist = jnp.abs(dist)
    dmat = jnp.where(dist[None] >= 0,
                     jnp.exp(adist[None] * lgf[:, None, None]),
                     jnp.exp(adist[None] * lgb[:, None, None]))

    scores = jnp.einsum('bcihd,bcjhd->bchij', q, k) * dmat.astype(dt)[None, None]
    o = jnp.einsum('bchij,bcjhe->bcihe', scores, v)

    qf = q * jnp.exp((idx + 1.0)[:, None] * lgf[None, :])[:, :, None].astype(dt)
    kf = k * jnp.exp((CHUNK - 1.0 - idx)[:, None] * lgf[None, :])[:, :, None].astype(dt)
    qb = q * jnp.exp((CHUNK - idx)[:, None] * lgb[None, :])[:, :, None].astype(dt)
    kb = k * jnp.exp(idx[:, None] * lgb[None, :])[:, :, None].astype(dt)
    cdf = jnp.exp(CHUNK * lgf)[None, :, None, None]
    cdb = jnp.exp(CHUNK * lgb)[None, :, None, None]

    def make_step(cd):
        def step(s, xs):
            qc, kc, vc = xs
            out = jnp.einsum('bihd,bhde->bihe', qc, s.astype(qc.dtype))
            s = s * cd + jnp.einsum('bjhd,bjhe->bhde', kc, vc).astype(jnp.float32)
            return s, out
        return step

    s0 = jnp.zeros((b, RET_HEADS, RET_DK, RET_DV), jnp.float32)
    tm = lambda t: jnp.moveaxis(t, 1, 0)
    _, of = lax.scan(make_step(cdf), s0, (tm(qf), tm(kf), tm(v)))
    _, ob = lax.scan(make_step(cdb), s0, (tm(qb), tm(kb), tm(v)), reverse=True)
    o = o + jnp.moveaxis(of + ob, 0, 1)
    o = o.reshape(b, lp, RET_HEADS, RET_DV)[:, pad:]

    of32 = o.astype(jnp.float32)
    mu = jnp.mean(of32, axis=-1, keepdims=True)
    var = jnp.mean(jnp.square(of32 - mu), axis=-1, keepdims=True)
    o = ((of32 - mu) * lax.rsqrt(var + EPS)).astype(h.dtype).reshape(b, L, hv)
    return (jax.nn.silu(g) * o) @ w_out


def conformer_conv(h, w_pw1, b_pw1, w_dw, b_dw, ln_g, ln_b, w_pw2, b_pw2):
    a = h @ w_pw1 + b_pw1
    u, gate = jnp.split(a, 2, axis=-1)
    u = u * jax.nn.sigmoid(gate)
    u = depthwise_conv(u, w_dw, b_dw)
    u = jax.nn.silu(layer_norm(u, ln_g, ln_b))
    return u @ w_pw2 + b_pw2


def conv_ffn(h, w_up, w_dw, b_dw, w_down):
    a, val = jnp.split(h @ w_up, 2, axis=-1)
    a = depthwise_conv(a, w_dw, b_dw)
    return (jax.nn.gelu(a, approximate=True) * val) @ w_down


def setup_inputs(seed: int = 0) -> dict:
    key = jax.random.key(seed)
    ks = jax.random.split(key, 24)
    f32 = jnp.float32
    nrm = lambda k, shape, s: jax.random.normal(k, shape, f32) * s
    D = D_MODEL
    hk, hv = RET_HEADS * RET_DK, RET_HEADS * RET_DV
    base = jnp.log(2.0 ** (5.0 + jnp.arange(RET_HEADS, dtype=f32)) - 1.0)
    return {
        "x_prompt": nrm(ks[0], (BATCH, SEQ, D), 1.0),
        "x_sample": nrm(ks[1], (DEC_BATCH, DEC_SEQ, D), 1.0),
        "meta_tokens": nrm(ks[2], (N_META, D), 1.0),
        "norm_pre_mix": 1.0 + nrm(ks[3], (DEPTH, D), 0.02),
        "norm_post_mix": 1.0 + nrm(ks[4], (DEPTH, D), 0.02),
        "norm_pre_ffn": 1.0 + nrm(ks[5], (DEPTH, D), 0.02),
        "norm_post_ffn": 1.0 + nrm(ks[6], (DEPTH, D), 0.02),
        "ret_w_in": nrm(ks[7], (N_RET, D, 2 * hk + 2 * hv), D ** -0.5),
        "ret_decay_fwd": base[None, :] + nrm(ks[8], (N_RET, RET_HEADS), 0.05),
        "ret_decay_bwd": base[None, :] + nrm(ks[9], (N_RET, RET_HEADS), 0.05),
        "ret_w_out": nrm(ks[10], (N_RET, hv, D), hv ** -0.5),
        "conv_w_pw1": nrm(ks[11], (N_CONV, D, 2 * D), D ** -0.5),
        "conv_b_pw1": nrm(ks[12], (N_CONV, 2 * D), 0.01),
        "conv_w_dw": nrm(ks[13], (N_CONV, CONV_K, D), CONV_K ** -0.5),
        "conv_b_dw": nrm(ks[14], (N_CONV, D), 0.01),
        "conv_ln_g": 1.0 + nrm(ks[15], (N_CONV, D), 0.02),
        "conv_ln_b": nrm(ks[16], (N_CONV, D), 0.01),
        "conv_w_pw2": nrm(ks[17], (N_CONV, D, D), D ** -0.5),
        "conv_b_pw2": nrm(ks[18], (N_CONV, D), 0.01),
        "ffn_w_up": nrm(ks[19], (DEPTH, D, 2 * FFN_HIDDEN), D ** -0.5),
        "ffn_w_dw": nrm(ks[20], (DEPTH, FFN_CONV_K, FFN_HIDDEN), FFN_CONV_K ** -0.5),
        "ffn_b_dw": nrm(ks[21], (DEPTH, FFN_HIDDEN), 0.01),
        "ffn_w_down": nrm(ks[22], (DEPTH, FFN_HIDDEN, D), FFN_HIDDEN ** -0.5),
    }


def reference(x_prompt, x_sample, meta_tokens, norm_pre_mix, norm_post_mix, norm_pre_ffn,
              norm_post_ffn, ret_w_in, ret_decay_fwd, ret_decay_bwd, ret_w_out,
              conv_w_pw1, conv_b_pw1, conv_w_dw, conv_b_dw, conv_ln_g, conv_ln_b,
              conv_w_pw2, conv_b_pw2, ffn_w_up, ffn_w_dw, ffn_b_dw, ffn_w_down):
    def encode(x):
        b = x.shape[0]
        meta = jnp.broadcast_to(meta_tokens[None].astype(x.dtype), (b, N_META, D_MODEL))
        h = jnp.concatenate([meta, x], axis=1)
        for i in range(DEPTH):
            j = i // N_MIXERS
            hn = rms_norm(h, norm_pre_mix[i])
            if i % N_MIXERS == 0:
                m = retention(hn, ret_w_in[j], ret_decay_fwd[j], ret_decay_bwd[j], ret_w_out[j])
            else:
                m = conformer_conv(hn, conv_w_pw1[j], conv_b_pw1[j], conv_w_dw[j], conv_b_dw[j],
                                   conv_ln_g[j], conv_ln_b[j], conv_w_pw2[j], conv_b_pw2[j])
            h = h + rms_norm(m, norm_post_mix[i])
            f = conv_ffn(rms_norm(h, norm_pre_ffn[i]), ffn_w_up[i], ffn_w_dw[i], ffn_b_dw[i],
                         ffn_w_down[i])
            h = h + rms_norm(f, norm_post_ffn[i])
        return h[:, N_META:]

    y_prompt = encode(x_prompt)
    y_sample = encode(x_sample)
    return (y_prompt, y_sample)
```

```python
import functools

import jax
import jax.numpy as jnp
from jax import lax
from jax.experimental import pallas as pl
from jax.experimental.pallas import tpu as pltpu

D_MODEL = 2048
DEPTH = 4
N_META = 16
N_MIXERS = 2
RET_HEADS = 8
RET_DK = D_MODEL // RET_HEADS
RET_DV = 2 * D_MODEL // RET_HEADS
HK = RET_HEADS * RET_DK
HV = RET_HEADS * RET_DV
CHUNK = 128
ROPE_BASE = 10000.0
CONV_K = 31
FFN_HIDDEN = 5504
FFN_CONV_K = 3
EPS = 1e-6

V7X_LANES = 128
V7X_VMEM_BYTES = 64 * 1024 * 1024
VMEM_LIMIT_CAP = 60000 * 1024

TM = 512
TM_DW = 256
HALO = 16
TN_PROJ = 1024
TK_OUT = 1024
TN_GLU = 512
TF = 512
FFN_PAD = -(-FFN_HIDDEN // TF) * TF
DW_COLS = 256
DW_ROWS = 64

F32 = jnp.float32
BF16 = jnp.bfloat16


def _params(est_bytes, n_axes):
    limit = min(VMEM_LIMIT_CAP, int(est_bytes * 1.25) + (4 << 20))
    return pltpu.CompilerParams(dimension_semantics=("arbitrary",) * n_axes,
                                vmem_limit_bytes=limit)


def _rms(x, g):
    return x * lax.rsqrt(jnp.mean(x * x, axis=-1, keepdims=True) + EPS) * g


def _keep_rows(tile, rows, chunks_per_seq):
    pad = CHUNK - N_META
    r = lax.broadcasted_iota(jnp.int32, (rows, 1), 0)
    sub = r // CHUNK
    in_pad = (r % CHUNK) < pad
    drop = jnp.zeros((rows, 1), jnp.bool_)
    for s in range(rows // CHUNK):
        first = lax.rem(tile * (rows // CHUNK) + s, chunks_per_seq) == 0
        drop = drop | ((sub == s) & in_pad & first)
    return jnp.where(drop, 0.0, 1.0).astype(F32)


def _proj_kernel(h_ref, g_ref, w_ref, cos_ref, sin_ref, o_ref, xn_ref):
    j = pl.program_id(1)

    @pl.when(j == 0)
    def _():
        xn_ref[...] = _rms(h_ref[...], g_ref[...]).astype(BF16)

    y = jnp.dot(xn_ref[...], w_ref[...], preferred_element_type=F32)
    n_rope = 2 * HK // TN_PROJ
    half = RET_DK // 2

    @pl.when(j < n_rope)
    def _():
        cos = cos_ref[...]
        sin = sin_ref[...]
        scale = jnp.where(j >= HK // TN_PROJ, RET_DK ** -0.5, 1.0).astype(F32)
        for hh in range(TN_PROJ // RET_DK):
            c0 = hh * RET_DK
            x1 = y[:, c0:c0 + half]
            x2 = y[:, c0 + half:c0 + RET_DK]
            o_ref[:, c0:c0 + half] = ((x1 * cos - x2 * sin) * scale).astype(BF16)
            o_ref[:, c0 + half:c0 + RET_DK] = ((x2 * cos + x1 * sin) * scale).astype(BF16)

    @pl.when(j >= n_rope)
    def _():
        o_ref[...] = y.astype(BF16)


def _proj(h, g, w, cos, sin):
    t = h.shape[0]
    n = w.shape[1]
    est = 2 * TM * D_MODEL * 4 + TM * D_MODEL * 2 + 2 * D_MODEL * TN_PROJ * 2 \
        + 2 * TM * TN_PROJ * 2 + 3 * TM * TN_PROJ * 4 + 4 * TM * V7X_LANES * 4
    return pl.pallas_call(
        _proj_kernel,
        grid=(t // TM, n // TN_PROJ),
        in_specs=[
            pl.BlockSpec((TM, D_MODEL), lambda i, j: (i, 0)),
            pl.BlockSpec((1, D_MODEL), lambda i, j: (0, 0)),
            pl.BlockSpec((D_MODEL, TN_PROJ), lambda i, j: (0, j)),
            pl.BlockSpec((TM, RET_DK // 2), lambda i, j: (i, 0)),
            pl.BlockSpec((TM, RET_DK // 2), lambda i, j: (i, 0)),
        ],
        out_specs=pl.BlockSpec((TM, TN_PROJ), lambda i, j: (i, j)),
        out_shape=jax.ShapeDtypeStruct((t, n), BF16),
        scratch_shapes=[pltpu.VMEM((TM, D_MODEL), BF16)],
        compiler_params=_params(est, 2),
        name="ret_proj",
    )(h, g, w, cos, sin)


def _ret_kernel(lg_ref, q_ref, k_ref, v_ref, g_ref, out_ref, o_ref, s_ref, *, nc):
    hd = pl.program_id(1)
    lgf = lg_ref[0, hd]
    lgb = lg_ref[1, hd]
    row = lax.broadcasted_iota(jnp.int32, (CHUNK, 1), 0).astype(F32)
    ii = lax.broadcasted_iota(jnp.int32, (CHUNK, CHUNK), 0)
    jj = lax.broadcasted_iota(jnp.int32, (CHUNK, CHUNK), 1)
    dist = (ii - jj).astype(F32)
    adist = jnp.abs(dist)
    dmat = jnp.where(dist >= 0, jnp.exp(adist * lgf), jnp.exp(adist * lgb))
    sq_f = jnp.exp((row + 1.0) * lgf)
    sk_f = jnp.exp((CHUNK - 1.0 - row) * lgf)
    sq_b = jnp.exp((CHUNK - row) * lgb)
    sk_b = jnp.exp(row * lgb)
    one = jnp.ones((1, 1), F32)
    cd_f = jnp.exp(one * (CHUNK * lgf))
    cd_b = jnp.exp(one * (CHUNK * lgb))
    contract_rows = (((0,), (0,)), ((), ()))
    contract_cols = (((1,), (1,)), ((), ()))

    def scaled(x, s):
        return (x.astype(F32) * s).astype(BF16)

    s_ref[...] = jnp.zeros_like(s_ref)

    def fwd(c, carry):
        r = pl.multiple_of(c * CHUNK, CHUNK)
        qc = q_ref[pl.ds(r, CHUNK), :]
        kc = k_ref[pl.ds(r, CHUNK), :]
        vc = v_ref[pl.ds(r, CHUNK), :]
        sc = lax.dot_general(qc, kc, contract_cols, preferred_element_type=F32) * dmat
        o = jnp.dot(sc.astype(BF16), vc, preferred_element_type=F32)
        s = s_ref[...]
        o = o + jnp.dot(scaled(qc, sq_f), s.astype(BF16), preferred_element_type=F32)
        o_ref[pl.ds(r, CHUNK), :] = o
        s_ref[...] = s * cd_f + lax.dot_general(scaled(kc, sk_f), vc, contract_rows,
                                                preferred_element_type=F32)
        return carry

    lax.fori_loop(0, nc, fwd, 0)

    s_ref[...] = jnp.zeros_like(s_ref)

    def bwd(t, carry):
        r = pl.multiple_of((nc - 1 - t) * CHUNK, CHUNK)
        qc = q_ref[pl.ds(r, CHUNK), :]
        kc = k_ref[pl.ds(r, CHUNK), :]
        vc = v_ref[pl.ds(r, CHUNK), :]
        s = s_ref[...]
        o = o_ref[pl.ds(r, CHUNK), :] + jnp.dot(scaled(qc, sq_b), s.astype(BF16),
                                                preferred_element_type=F32)
        mu = jnp.mean(o, axis=-1, keepdims=True)
        var = jnp.mean(jnp.square(o - mu), axis=-1, keepdims=True)
        on = (o - mu) * lax.rsqrt(var + EPS)
        gate = g_ref[pl.ds(r, CHUNK), :].astype(F32)
        out_ref[pl.ds(r, CHUNK), :] = (gate * jax.nn.sigmoid(gate) * on).astype(BF16)
        s_ref[...] = s * cd_b + lax.dot_general(scaled(kc, sk_b), vc, contract_rows,
                                                preferred_element_type=F32)
        return carry

    lax.fori_loop(0, nc, bwd, 0)


def _ret_core(proj, lg, batch, lp):
    t = proj.shape[0]
    est = 2 * (2 * lp * RET_DK * 2 + 3 * lp * RET_DV * 2) + lp * RET_DV * 4 \
        + RET_DK * RET_DV * 4 + (8 << 20)
    return pl.pallas_call(
        functools.partial(_ret_kernel, nc=lp // CHUNK),
        grid=(batch, RET_HEADS),
        in_specs=[
            pl.BlockSpec(memory_space=pltpu.SMEM),
            pl.BlockSpec((lp, RET_DK), lambda b, h: (b, h)),
            pl.BlockSpec((lp, RET_DK), lambda b, h: (b, HK // RET_DK + h)),
            pl.BlockSpec((lp, RET_DV), lambda b, h: (b, 2 * HK // RET_DV + h)),
            pl.BlockSpec((lp, RET_DV), lambda b, h: (b, (2 * HK + HV) // RET_DV + h)),
        ],
        out_specs=pl.BlockSpec((lp, RET_DV), lambda b, h: (b, h)),
        out_shape=jax.ShapeDtypeStruct((t, HV), BF16),
        scratch_shapes=[pltpu.VMEM((lp, RET_DV), F32), pltpu.VMEM((RET_DK, RET_DV), F32)],
        compiler_params=_params(est, 2),
        name="ret_core",
    )(lg, proj, proj, proj, proj)


def _outproj_kernel(x_ref, w_ref, h_ref, g_ref, o_ref):
    k = pl.program_id(1)
    part = jnp.dot(x_ref[...], w_ref[...], preferred_element_type=F32)

    @pl.when(k == 0)
    def _():
        o_ref[...] = part

    @pl.when(k > 0)
    def _():
        o_ref[...] += part

    @pl.when(k == pl.num_programs(1) - 1)
    def _():
        o_ref[...] = h_ref[...] + _rms(o_ref[...], g_ref[...])


def _outproj(x, w, h, g):
    t, kdim = x.shape
    est = 2 * TM * TK_OUT * 2 + 2 * TK_OUT * D_MODEL * 2 + 6 * TM * D_MODEL * 4
    return pl.pallas_call(
        _outproj_kernel,
        grid=(t // TM, kdim // TK_OUT),
        in_specs=[
            pl.BlockSpec((TM, TK_OUT), lambda i, k: (i, k)),
            pl.BlockSpec((TK_OUT, D_MODEL), lambda i, k: (k, 0)),
            pl.BlockSpec((TM, D_MODEL), lambda i, k: (i, 0)),
            pl.BlockSpec((1, D_MODEL), lambda i, k: (0, 0)),
        ],
        out_specs=pl.BlockSpec((TM, D_MODEL), lambda i, k: (i, 0)),
        out_shape=jax.ShapeDtypeStruct((t, D_MODEL), F32),
        compiler_params=_params(est, 2),
        name="ret_out",
    )(x, w, h, g)


def _ffn_kernel(h_ref, hp_ref, hn_ref, gpre_ref, gpost_ref, wa_ref, wv_ref, wdw_ref,
                bdw_ref, wd_ref, o_ref, xn_ref):
    i = pl.program_id(0)
    j = pl.program_id(1)

    @pl.when(j == 0)
    def _():
        g = gpre_ref[...]
        xn_ref[0:HALO, :] = _rms(hp_ref[...], g).astype(BF16)
        xn_ref[HALO:HALO + TM, :] = _rms(h_ref[...], g).astype(BF16)
        nxt = jnp.where(i == pl.num_programs(0) - 1, 0.0, _rms(hn_ref[...], g))
        xn_ref[HALO + TM:, :] = nxt.astype(BF16)

    a = jnp.dot(xn_ref[...], wa_ref[...], preferred_element_type=F32)
    val = jnp.dot(xn_ref[HALO:HALO + TM, :], wv_ref[...], preferred_element_type=F32)
    w = wdw_ref[...]
    ac = (w[0:1, :] * a[HALO - 1:HALO - 1 + TM, :] + w[1:2, :] * a[HALO:HALO + TM, :]
          + w[2:3, :] * a[HALO + 1:HALO + 1 + TM, :] + bdw_ref[...])
    hm = (jax.nn.gelu(ac, approximate=True) * val).astype(BF16)
    part = jnp.dot(hm, wd_ref[...], preferred_element_type=F32)

    @pl.when(j == 0)
    def _():
        o_ref[...] = part

    @pl.when(j > 0)
    def _():
        o_ref[...] += part

    @pl.when(j == pl.num_programs(1) - 1)
    def _():
        o_ref[...] = h_ref[...] + _rms(o_ref[...], gpost_ref[...])


def _ffn(h, gpre, gpost, w_up, w_dw, b_dw, w_down):
    t = h.shape[0]
    nf = FFN_PAD // TF
    nh = t // HALO
    est = 6 * TM * D_MODEL * 4 + (TM + 2 * HALO) * D_MODEL * 2 + 6 * D_MODEL * TF * 2 \
        + 6 * (TM + 2 * HALO) * TF * 4
    return pl.pallas_call(
        _ffn_kernel,
        grid=(t // TM, nf),
        in_specs=[
            pl.BlockSpec((TM, D_MODEL), lambda i, j: (i, 0)),
            pl.BlockSpec((HALO, D_MODEL), lambda i, j: (jnp.maximum(i * (TM // HALO) - 1, 0), 0)),
            pl.BlockSpec((HALO, D_MODEL),
                         lambda i, j: (jnp.minimum((i + 1) * (TM // HALO), nh - 1), 0)),
            pl.BlockSpec((1, D_MODEL), lambda i, j: (0, 0)),
            pl.BlockSpec((1, D_MODEL), lambda i, j: (0, 0)),
            pl.BlockSpec((D_MODEL, TF), lambda i, j: (0, j)),
            pl.BlockSpec((D_MODEL, TF), lambda i, j: (0, nf + j)),
            pl.BlockSpec((FFN_CONV_K, TF), lambda i, j: (0, j)),
            pl.BlockSpec((1, TF), lambda i, j: (0, j)),
            pl.BlockSpec((TF, D_MODEL), lambda i, j: (j, 0)),
        ],
        out_specs=pl.BlockSpec((TM, D_MODEL), lambda i, j: (i, 0)),
        out_shape=jax.ShapeDtypeStruct((t, D_MODEL), F32),
        scratch_shapes=[pltpu.VMEM((TM + 2 * HALO, D_MODEL), BF16)],
        compiler_params=_params(est, 2),
        name="conv_ffn",
    )(h, h, h, gpre, gpost, w_up, w_up, w_dw, b_dw, w_down)


def _glu_kernel(h_ref, g_ref, wu_ref, wg_ref, bu_ref, bg_ref, o_ref, xn_ref, *, chunks_per_seq):
    i = pl.program_id(0)
    j = pl.program_id(1)

    @pl.when(j == 0)
    def _():
        xn_ref[...] = _rms(h_ref[...], g_ref[...]).astype(BF16)

    xn = xn_ref[...]
    u = jnp.dot(xn, wu_ref[...], preferred_element_type=F32) + bu_ref[...]
    gate = jnp.dot(xn, wg_ref[...], preferred_element_type=F32) + bg_ref[...]
    keep = _keep_rows(i, TM, chunks_per_seq)
    o_ref[...] = (u * jax.nn.sigmoid(gate) * keep).astype(BF16)


def _glu(h, g, w, b, chunks_per_seq):
    t = h.shape[0]
    nn = D_MODEL // TN_GLU
    est = 2 * TM * D_MODEL * 4 + TM * D_MODEL * 2 + 4 * D_MODEL * TN_GLU * 2 \
        + 2 * TM * TN_GLU * 2 + 4 * TM * TN_GLU * 4
    return pl.pallas_call(
        functools.partial(_glu_kernel, chunks_per_seq=chunks_per_seq),
        grid=(t // TM, nn),
        in_specs=[
            pl.BlockSpec((TM, D_MODEL), lambda i, j: (i, 0)),
            pl.BlockSpec((1, D_MODEL), lambda i, j: (0, 0)),
            pl.BlockSpec((D_MODEL, TN_GLU), lambda i, j: (0, j)),
            pl.BlockSpec((D_MODEL, TN_GLU), lambda i, j: (0, nn + j)),
            pl.BlockSpec((1, TN_GLU), lambda i, j: (0, j)),
            pl.BlockSpec((1, TN_GLU), lambda i, j: (0, nn + j)),
        ],
        out_specs=pl.BlockSpec((TM, TN_GLU), lambda i, j: (i, j)),
        out_shape=jax.ShapeDtypeStruct((t, D_MODEL), BF16),
        scratch_shapes=[pltpu.VMEM((TM, D_MODEL), BF16)],
        compiler_params=_params(est, 2),
        name="conf_glu",
    )(h, g, w, w, b, b)


def _dw_kernel(u_ref, up_ref, un_ref, wdw_ref, bdw_ref, lng_ref, lnb_ref, w2_ref, b2_ref,
               h_ref, gpost_ref, o_ref, ue_ref, cv_ref, y_ref, *, chunks_per_seq):
    i = pl.program_id(0)
    ue_ref[0:HALO, :] = up_ref[...].astype(F32)
    ue_ref[HALO:HALO + TM_DW, :] = u_ref[...].astype(F32)
    nxt = jnp.where(i == pl.num_programs(0) - 1, 0.0, un_ref[...].astype(F32))
    ue_ref[HALO + TM_DW:, :] = nxt

    shift = HALO - (CONV_K - 1) // 2

    def strip(ci, carry):
        c0 = pl.multiple_of(ci * DW_COLS, DW_COLS)
        w = wdw_ref[:, pl.ds(c0, DW_COLS)]
        bias = bdw_ref[:, pl.ds(c0, DW_COLS)]
        for rb in range(TM_DW // DW_ROWS):
            r0 = rb * DW_ROWS
            acc = jnp.zeros((DW_ROWS, DW_COLS), F32)
            for k in range(CONV_K):
                acc = acc + w[k:k + 1, :] * ue_ref[r0 + shift + k:r0 + shift + k + DW_ROWS,
                                                   pl.ds(c0, DW_COLS)]
            cv_ref[r0:r0 + DW_ROWS, pl.ds(c0, DW_COLS)] = acc + bias
        return carry

    lax.fori_loop(0, D_MODEL // DW_COLS, strip, 0)

    for rb in range(TM_DW // DW_ROWS):
        r0 = rb * DW_ROWS
        x = cv_ref[r0:r0 + DW_ROWS, :]
        mu = jnp.mean(x, axis=-1, keepdims=True)
        var = jnp.mean(jnp.square(x - mu), axis=-1, keepdims=True)
        ln = (x - mu) * lax.rsqrt(var + EPS) * lng_ref[...] + lnb_ref[...]
        y_ref[r0:r0 + DW_ROWS, :] = (ln * jax.nn.sigmoid(ln)).astype(BF16)

    m = jnp.dot(y_ref[...], w2_ref[...], preferred_element_type=F32) + b2_ref[...]
    keep = _keep_rows(i, TM_DW, chunks_per_seq)
    o_ref[...] = h_ref[...] + _rms(m, gpost_ref[...]) * keep


def _dwconv(u, w_dw, b_dw, ln_g, ln_b, w2, b2, h, gpost, chunks_per_seq):
    t = u.shape[0]
    nh = t // HALO
    est = 2 * TM_DW * D_MODEL * 2 + (TM_DW + 2 * HALO) * D_MODEL * 4 + TM_DW * D_MODEL * 6 \
        + 2 * D_MODEL * D_MODEL * 2 + 4 * TM_DW * D_MODEL * 4 + 4 * TM_DW * D_MODEL * 4
    vec = pl.BlockSpec((1, D_MODEL), lambda i: (0, 0))
    return pl.pallas_call(
        functools.partial(_dw_kernel, chunks_per_seq=chunks_per_seq),
        grid=(t // TM_DW,),
        in_specs=[
            pl.BlockSpec((TM_DW, D_MODEL), lambda i: (i, 0)),
            pl.BlockSpec((HALO, D_MODEL), lambda i: (jnp.maximum(i * (TM_DW // HALO) - 1, 0), 0)),
            pl.BlockSpec((HALO, D_MODEL),
                         lambda i: (jnp.minimum((i + 1) * (TM_DW // HALO), nh - 1), 0)),
            pl.BlockSpec((CONV_K, D_MODEL), lambda i: (0, 0)),
            vec, vec, vec,
            pl.BlockSpec((D_MODEL, D_MODEL), lambda i: (0, 0)),
            vec,
            pl.BlockSpec((TM_DW, D_MODEL), lambda i: (i, 0)),
            vec,
        ],
        out_specs=pl.BlockSpec((TM_DW, D_MODEL), lambda i: (i, 0)),
        out_shape=jax.ShapeDtypeStruct((t, D_MODEL), F32),
        scratch_shapes=[
            pltpu.VMEM((TM_DW + 2 * HALO, D_MODEL), F32),
            pltpu.VMEM((TM_DW, D_MODEL), F32),
            pltpu.VMEM((TM_DW, D_MODEL), BF16),
        ],
        compiler_params=_params(est, 1),
        name="conf_dw",
    )(u, u, u, w_dw, b_dw, ln_g, ln_b, w2, b2, h, gpost)


def _row(v):
    return v.reshape(1, -1).astype(F32)


def _encode(x, meta_tokens, p):
    batch, seq, _ = x.shape
    pad = (-(seq + N_META)) % CHUNK
    lp = seq + N_META + pad
    cps = lp // CHUNK
    meta = jnp.broadcast_to(meta_tokens[None].astype(x.dtype), (batch, N_META, D_MODEL))
    h = jnp.concatenate([jnp.zeros((batch, pad, D_MODEL), x.dtype), meta, x], axis=1)
    h = h.reshape(batch * lp, D_MODEL)

    half = RET_DK // 2
    inv = ROPE_BASE ** (-jnp.arange(half, dtype=F32) / half)
    ang = jnp.arange(lp).astype(F32)[:, None] * inv[None, :]
    cos = jnp.tile(jnp.cos(ang), (batch, 1))
    sin = jnp.tile(jnp.sin(ang), (batch, 1))

    for i in range(DEPTH):
        j = i // N_MIXERS
        if i % N_MIXERS == 0:
            proj = _proj(h, p["pre_mix"][i], p["ret_w_in"][j], cos, sin)
            gated = _ret_core(proj, p["ret_lg"][j], batch, lp)
            h = _outproj(gated, p["ret_w_out"][j], h, p["post_mix"][i])
        else:
            u = _glu(h, p["pre_mix"][i], p["conv_w_pw1"][j], p["conv_b_pw1"][j], cps)
            h = _dwconv(u, p["conv_w_dw"][j], p["conv_b_dw"][j], p["conv_ln_g"][j],
                        p["conv_ln_b"][j], p["conv_w_pw2"][j], p["conv_b_pw2"][j], h,
                        p["post_mix"][i], cps)
        h = _ffn(h, p["pre_ffn"][i], p["post_ffn"][i], p["ffn_w_up"][i], p["ffn_w_dw"][i],
                 p["ffn_b_dw"][i], p["ffn_w_down"][i])
    return h.reshape(batch, lp, D_MODEL)[:, pad + N_META:]


def kernel(x_prompt, x_sample, meta_tokens, norm_pre_mix, norm_post_mix, norm_pre_ffn, norm_post_ffn, ret_w_in, ret_decay_fwd, ret_decay_bwd, ret_w_out, conv_w_pw1, conv_b_pw1, conv_w_dw, conv_b_dw, conv_ln_g, conv_ln_b, conv_w_pw2, conv_b_pw2, ffn_w_up, ffn_w_dw, ffn_b_dw, ffn_w_down):
    fpad = FFN_PAD - FFN_HIDDEN
    w_up = ffn_w_up.astype(BF16)
    w_up = jnp.concatenate(
        [jnp.pad(w_up[:, :, :FFN_HIDDEN], ((0, 0), (0, 0), (0, fpad))),
         jnp.pad(w_up[:, :, FFN_HIDDEN:], ((0, 0), (0, 0), (0, fpad)))], axis=-1)
    p = {
        "pre_mix": [_row(v) for v in norm_pre_mix],
        "post_mix": [_row(v) for v in norm_post_mix],
        "pre_ffn": [_row(v) for v in norm_pre_ffn],
        "post_ffn": [_row(v) for v in norm_post_ffn],
        "ret_w_in": ret_w_in.astype(BF16),
        "ret_lg": jnp.stack([jax.nn.log_sigmoid(ret_decay_fwd.astype(F32)),
                             jax.nn.log_sigmoid(ret_decay_bwd.astype(F32))], axis=1),
        "ret_w_out": ret_w_out.astype(BF16),
        "conv_w_pw1": conv_w_pw1.astype(BF16),
        "conv_b_pw1": [_row(v) for v in conv_b_pw1],
        "conv_w_dw": conv_w_dw.astype(F32),
        "conv_b_dw": [_row(v) for v in conv_b_dw],
        "conv_ln_g": [_row(v) for v in conv_ln_g],
        "conv_ln_b": [_row(v) for v in conv_ln_b],
        "conv_w_pw2": conv_w_pw2.astype(BF16),
        "conv_b_pw2": [_row(v) for v in conv_b_pw2],
        "ffn_w_up": w_up,
        "ffn_w_dw": jnp.pad(ffn_w_dw.astype(F32), ((0, 0), (0, 0), (0, fpad))),
        "ffn_b_dw": [_row(v) for v in jnp.pad(ffn_b_dw, ((0, 0), (0, fpad)))],
        "ffn_w_down": jnp.pad(ffn_w_down.astype(BF16), ((0, 0), (0, fpad), (0, 0))),
    }
    return (_encode(x_prompt, meta_tokens, p), _encode(x_sample, meta_tokens, p))
```

```python
import functools

import jax
import jax.numpy as jnp
from jax import lax
from jax.experimental import pallas as pl
from jax.experimental.pallas import tpu as pltpu

D_MODEL = 2048
DEPTH = 4
N_META = 16
N_MIXERS = 2
RET_HEADS = 8
RET_DK = D_MODEL // RET_HEADS
RET_DV = 2 * D_MODEL // RET_HEADS
HK = RET_HEADS * RET_DK
HV = RET_HEADS * RET_DV
CHUNK = 128
ROPE_BASE = 10000.0
CONV_K = 31
FFN_HIDDEN = 5504
FFN_CONV_K = 3
EPS = 1e-6

V7X_LANES = 128
V7X_VMEM_BYTES = 64 * 1024 * 1024
VMEM_LIMIT_CAP = 60000 * 1024

TM = 512
TM_DW = 256
HALO = 16
TN_PROJ = 2048
TK_OUT = 2048
TN_GLU = 1024
TF = 512
TD = 512
FFN_PAD = -(-FFN_HIDDEN // TF) * TF
NF = FFN_PAD // TF
DW_SLABS = D_MODEL // V7X_LANES
DW_ROWS = 64

F32 = jnp.float32
BF16 = jnp.bfloat16


def _params(est_bytes, n_axes):
    limit = min(VMEM_LIMIT_CAP, int(est_bytes * 1.25) + (4 << 20))
    return pltpu.CompilerParams(dimension_semantics=("arbitrary",) * n_axes,
                                vmem_limit_bytes=limit)


def _rms(x, g):
    return x * lax.rsqrt(jnp.mean(x * x, axis=-1, keepdims=True) + EPS) * g


def _keep_rows(tile, rows, chunks_per_seq):
    pad = CHUNK - N_META
    r = lax.broadcasted_iota(jnp.int32, (rows, 1), 0)
    sub = r // CHUNK
    in_pad = (r % CHUNK) < pad
    drop = jnp.zeros((rows, 1), jnp.bool_)
    for s in range(rows // CHUNK):
        first = lax.rem(tile * (rows // CHUNK) + s, chunks_per_seq) == 0
        drop = drop | ((sub == s) & in_pad & first)
    return jnp.where(drop, 0.0, 1.0).astype(F32)


def _proj_kernel(h_ref, g_ref, w_ref, cos_ref, sin_ref, o_ref, xn_ref):
    j = pl.program_id(1)

    @pl.when(j == 0)
    def _():
        xn_ref[...] = _rms(h_ref[...], g_ref[...]).astype(BF16)

    y = jnp.dot(xn_ref[...], w_ref[...], preferred_element_type=F32)
    n_rope = 2 * HK // TN_PROJ
    half = RET_DK // 2

    @pl.when(j < n_rope)
    def _():
        cos = cos_ref[...]
        sin = sin_ref[...]
        scale = jnp.where(j >= HK // TN_PROJ, RET_DK ** -0.5, 1.0).astype(F32)
        for hh in range(TN_PROJ // RET_DK):
            c0 = hh * RET_DK
            x1 = y[:, c0:c0 + half]
            x2 = y[:, c0 + half:c0 + RET_DK]
            o_ref[:, c0:c0 + half] = ((x1 * cos - x2 * sin) * scale).astype(BF16)
            o_ref[:, c0 + half:c0 + RET_DK] = ((x2 * cos + x1 * sin) * scale).astype(BF16)

    @pl.when(j >= n_rope)
    def _():
        o_ref[...] = y.astype(BF16)


def _proj(h, g, w, cos, sin):
    t = h.shape[0]
    n = w.shape[1]
    est = 2 * TM * D_MODEL * 4 + TM * D_MODEL * 2 + 2 * D_MODEL * TN_PROJ * 2 \
        + 2 * TM * TN_PROJ * 2 + 3 * TM * TN_PROJ * 4 + 4 * TM * V7X_LANES * 4
    return pl.pallas_call(
        _proj_kernel,
        grid=(t // TM, n // TN_PROJ),
        in_specs=[
            pl.BlockSpec((TM, D_MODEL), lambda i, j: (i, 0)),
            pl.BlockSpec((1, D_MODEL), lambda i, j: (0, 0)),
            pl.BlockSpec((D_MODEL, TN_PROJ), lambda i, j: (0, j)),
            pl.BlockSpec((TM, RET_DK // 2), lambda i, j: (i, 0)),
            pl.BlockSpec((TM, RET_DK // 2), lambda i, j: (i, 0)),
        ],
        out_specs=pl.BlockSpec((TM, TN_PROJ), lambda i, j: (i, j)),
        out_shape=jax.ShapeDtypeStruct((t, n), BF16),
        scratch_shapes=[pltpu.VMEM((TM, D_MODEL), BF16)],
        compiler_params=_params(est, 2),
        name="ret_proj",
    )(h, g, w, cos, sin)


def _ret_kernel(lg_ref, q_ref, k_ref, v_ref, g_ref, out_ref, o_ref, sf_ref, sb_ref, *, nc):
    hd = pl.program_id(1)
    lgf = lg_ref[0, hd]
    lgb = lg_ref[1, hd]
    row = lax.broadcasted_iota(jnp.int32, (CHUNK, 1), 0).astype(F32)
    ii = lax.broadcasted_iota(jnp.int32, (CHUNK, CHUNK), 0)
    jj = lax.broadcasted_iota(jnp.int32, (CHUNK, CHUNK), 1)
    dist = (ii - jj).astype(F32)
    adist = jnp.abs(dist)
    dmat = jnp.where(dist >= 0, jnp.exp(adist * lgf), jnp.exp(adist * lgb))
    sq_f = jnp.exp((row + 1.0) * lgf)
    sk_f = jnp.exp((CHUNK - 1.0 - row) * lgf)
    sq_b = jnp.exp((CHUNK - row) * lgb)
    sk_b = jnp.exp(row * lgb)
    one = jnp.ones((1, 1), F32)
    cd_f = jnp.exp(one * (CHUNK * lgf))
    cd_b = jnp.exp(one * (CHUNK * lgb))
    contract_rows = (((0,), (0,)), ((), ()))
    contract_cols = (((1,), (1,)), ((), ()))

    def scaled(x, s):
        return (x.astype(F32) * s).astype(BF16)

    o_ref[...] = jnp.zeros_like(o_ref)
    sf_ref[...] = jnp.zeros_like(sf_ref)
    sb_ref[...] = jnp.zeros_like(sb_ref)

    def step(t, carry):
        r = pl.multiple_of(t * CHUNK, CHUNK)
        qc = q_ref[pl.ds(r, CHUNK), :]
        kc = k_ref[pl.ds(r, CHUNK), :]
        vc = v_ref[pl.ds(r, CHUNK), :]
        sc = lax.dot_general(qc, kc, contract_cols, preferred_element_type=F32) * dmat
        sf = sf_ref[...]
        o_ref[pl.ds(r, CHUNK), :] += (
            jnp.dot(sc.astype(BF16), vc, preferred_element_type=F32)
            + jnp.dot(scaled(qc, sq_f), sf.astype(BF16), preferred_element_type=F32))
        sf_ref[...] = sf * cd_f + lax.dot_general(scaled(kc, sk_f), vc, contract_rows,
                                                  preferred_element_type=F32)

        r = pl.multiple_of((nc - 1 - t) * CHUNK, CHUNK)
        qc = q_ref[pl.ds(r, CHUNK), :]
        kc = k_ref[pl.ds(r, CHUNK), :]
        vc = v_ref[pl.ds(r, CHUNK), :]
        sb = sb_ref[...]
        o_ref[pl.ds(r, CHUNK), :] += jnp.dot(scaled(qc, sq_b), sb.astype(BF16),
                                             preferred_element_type=F32)
        sb_ref[...] = sb * cd_b + lax.dot_general(scaled(kc, sk_b), vc, contract_rows,
                                                  preferred_element_type=F32)
        return carry

    lax.fori_loop(0, nc, step, 0)

    def finish(c, carry):
        r = pl.multiple_of(c * CHUNK, CHUNK)
        o = o_ref[pl.ds(r, CHUNK), :]
        mu = jnp.mean(o, axis=-1, keepdims=True)
        var = jnp.mean(jnp.square(o - mu), axis=-1, keepdims=True)
        on = (o - mu) * lax.rsqrt(var + EPS)
        gate = g_ref[pl.ds(r, CHUNK), :].astype(F32)
        out_ref[pl.ds(r, CHUNK), :] = (gate * jax.nn.sigmoid(gate) * on).astype(BF16)
        return carry

    lax.fori_loop(0, nc, finish, 0)


def _ret_core(proj, lg, batch, lp):
    t = proj.shape[0]
    est = 2 * (2 * lp * RET_DK * 2 + 3 * lp * RET_DV * 2) + lp * RET_DV * 4 \
        + 2 * RET_DK * RET_DV * 4 + (8 << 20)
    return pl.pallas_call(
        functools.partial(_ret_kernel, nc=lp // CHUNK),
        grid=(batch, RET_HEADS),
        in_specs=[
            pl.BlockSpec(memory_space=pltpu.SMEM),
            pl.BlockSpec((lp, RET_DK), lambda b, h: (b, h)),
            pl.BlockSpec((lp, RET_DK), lambda b, h: (b, HK // RET_DK + h)),
            pl.BlockSpec((lp, RET_DV), lambda b, h: (b, 2 * HK // RET_DV + h)),
            pl.BlockSpec((lp, RET_DV), lambda b, h: (b, (2 * HK + HV) // RET_DV + h)),
        ],
        out_specs=pl.BlockSpec((lp, RET_DV), lambda b, h: (b, h)),
        out_shape=jax.ShapeDtypeStruct((t, HV), BF16),
        scratch_shapes=[pltpu.VMEM((lp, RET_DV), F32), pltpu.VMEM((RET_DK, RET_DV), F32),
                        pltpu.VMEM((RET_DK, RET_DV), F32)],
        compiler_params=_params(est, 2),
        name="ret_core",
    )(lg, proj, proj, proj, proj)


def _outproj_kernel(x_ref, w_ref, h_ref, g_ref, o_ref):
    k = pl.program_id(1)
    part = jnp.dot(x_ref[...], w_ref[...], preferred_element_type=F32)

    @pl.when(k == 0)
    def _():
        o_ref[...] = part

    @pl.when(k > 0)
    def _():
        o_ref[...] += part

    @pl.when(k == pl.num_programs(1) - 1)
    def _():
        o_ref[...] = h_ref[...] + _rms(o_ref[...], g_ref[...])


def _outproj(x, w, h, g):
    t, kdim = x.shape
    est = 2 * TM * TK_OUT * 2 + 2 * TK_OUT * D_MODEL * 2 + 6 * TM * D_MODEL * 4
    return pl.pallas_call(
        _outproj_kernel,
        grid=(t // TM, kdim // TK_OUT),
        in_specs=[
            pl.BlockSpec((TM, TK_OUT), lambda i, k: (i, k)),
            pl.BlockSpec((TK_OUT, D_MODEL), lambda i, k: (k, 0)),
            pl.BlockSpec((TM, D_MODEL), lambda i, k: (i, 0)),
            pl.BlockSpec((1, D_MODEL), lambda i, k: (0, 0)),
        ],
        out_specs=pl.BlockSpec((TM, D_MODEL), lambda i, k: (i, 0)),
        out_shape=jax.ShapeDtypeStruct((t, D_MODEL), F32),
        compiler_params=_params(est, 2),
        name="ret_out",
    )(x, w, h, g)


def _ffn_kernel(h_ref, hp_ref, hn_ref, gpre_ref, gpost_ref, wa_ref, wv_ref, wdw_ref,
                bdw_ref, wd_ref, o_ref, xn_ref, hm_ref):
    i = pl.program_id(0)
    j = pl.program_id(1)

    @pl.when(j == 0)
    def _():
        g = gpre_ref[...]
        xn_ref[0:HALO, :] = _rms(hp_ref[...], g).astype(BF16)
        xn_ref[HALO:HALO + TM, :] = _rms(h_ref[...], g).astype(BF16)
        nxt = jnp.where(i == pl.num_programs(0) - 1, 0.0, _rms(hn_ref[...], g))
        xn_ref[HALO + TM:, :] = nxt.astype(BF16)

    @pl.when(j < NF)
    def _():
        a = jnp.dot(xn_ref[...], wa_ref[...], preferred_element_type=F32)
        val = jnp.dot(xn_ref[HALO:HALO + TM, :], wv_ref[...], preferred_element_type=F32)
        w = wdw_ref[...]
        ac = (w[0:1, :] * a[HALO - 1:HALO - 1 + TM, :] + w[1:2, :] * a[HALO:HALO + TM, :]
              + w[2:3, :] * a[HALO + 1:HALO + 1 + TM, :] + bdw_ref[...])
        hm = (jax.nn.gelu(ac, approximate=True) * val).astype(BF16)
        hm_ref[:, pl.ds(pl.multiple_of(j * TF, TF), TF)] = hm

    @pl.when(j >= NF)
    def _():
        f = jnp.dot(hm_ref[...], wd_ref[...], preferred_element_type=F32)
        o_ref[:, pl.ds(pl.multiple_of((j - NF) * TD, TD), TD)] = f

    @pl.when(j == pl.num_programs(1) - 1)
    def _():
        o_ref[...] = h_ref[...] + _rms(o_ref[...], gpost_ref[...])


def _ffn(h, gpre, gpost, w_up, w_dw, b_dw, w_down):
    t = h.shape[0]
    nd = D_MODEL // TD
    nh = t // HALO
    ext = TM + 2 * HALO
    est = 4 * TM * D_MODEL * 4 + ext * D_MODEL * 2 + TM * FFN_PAD * 2 \
        + 4 * D_MODEL * TF * 2 + 2 * FFN_PAD * TD * 2 + 6 * ext * TF * 4 \
        + 2 * TM * D_MODEL * 4
    up_col = lambda i, j: (0, jnp.minimum(j, NF - 1))
    return pl.pallas_call(
        _ffn_kernel,
        grid=(t // TM, NF + nd),
        in_specs=[
            pl.BlockSpec((TM, D_MODEL), lambda i, j: (i, 0)),
            pl.BlockSpec((HALO, D_MODEL), lambda i, j: (jnp.maximum(i * (TM // HALO) - 1, 0), 0)),
            pl.BlockSpec((HALO, D_MODEL),
                         lambda i, j: (jnp.minimum((i + 1) * (TM // HALO), nh - 1), 0)),
            pl.BlockSpec((1, D_MODEL), lambda i, j: (0, 0)),
            pl.BlockSpec((1, D_MODEL), lambda i, j: (0, 0)),
            pl.BlockSpec((D_MODEL, TF), up_col),
            pl.BlockSpec((D_MODEL, TF), lambda i, j: (0, NF + jnp.minimum(j, NF - 1))),
            pl.BlockSpec((FFN_CONV_K, TF), up_col),
            pl.BlockSpec((1, TF), up_col),
            pl.BlockSpec((FFN_PAD, TD), lambda i, j: (0, jnp.clip(j - NF, 0, nd - 1))),
        ],
        out_specs=pl.BlockSpec((TM, D_MODEL), lambda i, j: (i, 0)),
        out_shape=jax.ShapeDtypeStruct((t, D_MODEL), F32),
        scratch_shapes=[pltpu.VMEM((ext, D_MODEL), BF16),
                        pltpu.VMEM((TM, FFN_PAD), BF16)],
        compiler_params=_params(est, 2),
        name="conv_ffn",
    )(h, h, h, gpre, gpost, w_up, w_up, w_dw, b_dw, w_down)


def _glu_kernel(h_ref, g_ref, wu_ref, wg_ref, bu_ref, bg_ref, o_ref, xn_ref, *, chunks_per_seq):
    i = pl.program_id(0)
    j = pl.program_id(1)

    @pl.when(j == 0)
    def _():
        xn_ref[...] = _rms(h_ref[...], g_ref[...]).astype(BF16)

    xn = xn_ref[...]
    u = jnp.dot(xn, wu_ref[...], preferred_element_type=F32) + bu_ref[...]
    gate = jnp.dot(xn, wg_ref[...], preferred_element_type=F32) + bg_ref[...]
    keep = _keep_rows(i, TM, chunks_per_seq)
    o_ref[...] = (u * jax.nn.sigmoid(gate) * keep).astype(BF16)


def _glu(h, g, w, b, chunks_per_seq):
    t = h.shape[0]
    nn = D_MODEL // TN_GLU
    est = 2 * TM * D_MODEL * 4 + TM * D_MODEL * 2 + 4 * D_MODEL * TN_GLU * 2 \
        + 2 * TM * TN_GLU * 2 + 4 * TM * TN_GLU * 4
    return pl.pallas_call(
        functools.partial(_glu_kernel, chunks_per_seq=chunks_per_seq),
        grid=(t // TM, nn),
        in_specs=[
            pl.BlockSpec((TM, D_MODEL), lambda i, j: (i, 0)),
            pl.BlockSpec((1, D_MODEL), lambda i, j: (0, 0)),
            pl.BlockSpec((D_MODEL, TN_GLU), lambda i, j: (0, j)),
            pl.BlockSpec((D_MODEL, TN_GLU), lambda i, j: (0, nn + j)),
            pl.BlockSpec((1, TN_GLU), lambda i, j: (0, j)),
            pl.BlockSpec((1, TN_GLU), lambda i, j: (0, nn + j)),
        ],
        out_specs=pl.BlockSpec((TM, TN_GLU), lambda i, j: (i, j)),
        out_shape=jax.ShapeDtypeStruct((t, D_MODEL), BF16),
        scratch_shapes=[pltpu.VMEM((TM, D_MODEL), BF16)],
        compiler_params=_params(est, 2),
        name="conf_glu",
    )(h, g, w, w, b, b)


def _dw_kernel(u_ref, up_ref, un_ref, wdw_ref, bdw_ref, lng_ref, lnb_ref, w2_ref, b2_ref,
               h_ref, gpost_ref, o_ref, ue_ref, cv_ref, y_ref, *, chunks_per_seq):
    i = pl.program_id(0)
    last = i == pl.num_programs(0) - 1
    for s in range(DW_SLABS):
        lanes = slice(s * V7X_LANES, (s + 1) * V7X_LANES)
        ue_ref[s, 0:HALO, :] = up_ref[:, lanes].astype(F32)
        ue_ref[s, HALO:HALO + TM_DW, :] = u_ref[:, lanes].astype(F32)
        ue_ref[s, HALO + TM_DW:, :] = jnp.where(last, 0.0, un_ref[:, lanes].astype(F32))

    shift = HALO - (CONV_K - 1) // 2

    def slab(s, carry):
        w = wdw_ref[s]
        bias = bdw_ref[s]
        for rb in range(TM_DW // DW_ROWS):
            r0 = rb * DW_ROWS
            acc = jnp.zeros((DW_ROWS, V7X_LANES), F32)
            for k in range(CONV_K):
                acc = acc + w[k:k + 1, :] * ue_ref[s, r0 + shift + k:r0 + shift + k + DW_ROWS, :]
            cv_ref[s, r0:r0 + DW_ROWS, :] = acc + bias
        return carry

    lax.fori_loop(0, DW_SLABS, slab, 0)

    for rb in range(TM_DW // DW_ROWS):
        rows = slice(rb * DW_ROWS, (rb + 1) * DW_ROWS)
        tot = cv_ref[0, rows, :]
        for s in range(1, DW_SLABS):
            tot = tot + cv_ref[s, rows, :]
        mu = jnp.sum(tot, axis=-1, keepdims=True) * (1.0 / D_MODEL)
        sq = jnp.square(cv_ref[0, rows, :] - mu)
        for s in range(1, DW_SLABS):
            sq = sq + jnp.square(cv_ref[s, rows, :] - mu)
        inv = lax.rsqrt(jnp.sum(sq, axis=-1, keepdims=True) * (1.0 / D_MODEL) + EPS)
        for s in range(DW_SLABS):
            lanes = slice(s * V7X_LANES, (s + 1) * V7X_LANES)
            ln = (cv_ref[s, rows, :] - mu) * inv * lng_ref[:, lanes] + lnb_ref[:, lanes]
            y_ref[rows, lanes] = (ln * jax.nn.sigmoid(ln)).astype(BF16)

    m = jnp.dot(y_ref[...], w2_ref[...], preferred_element_type=F32) + b2_ref[...]
    keep = _keep_rows(i, TM_DW, chunks_per_seq)
    o_ref[...] = h_ref[...] + _rms(m, gpost_ref[...]) * keep


def _dwconv(u, w_dw, b_dw, ln_g, ln_b, w2, b2, h, gpost, chunks_per_seq):
    t = u.shape[0]
    nh = t // HALO
    est = 2 * TM_DW * D_MODEL * 2 + (TM_DW + 2 * HALO) * D_MODEL * 4 + TM_DW * D_MODEL * 6 \
        + 2 * D_MODEL * D_MODEL * 2 + 4 * TM_DW * D_MODEL * 4 + 4 * TM_DW * D_MODEL * 4
    vec = pl.BlockSpec((1, D_MODEL), lambda i: (0, 0))
    return pl.pallas_call(
        functools.partial(_dw_kernel, chunks_per_seq=chunks_per_seq),
        grid=(t // TM_DW,),
        in_specs=[
            pl.BlockSpec((TM_DW, D_MODEL), lambda i: (i, 0)),
            pl.BlockSpec((HALO, D_MODEL), lambda i: (jnp.maximum(i * (TM_DW // HALO) - 1, 0), 0)),
            pl.BlockSpec((HALO, D_MODEL),
                         lambda i: (jnp.minimum((i + 1) * (TM_DW // HALO), nh - 1), 0)),
            pl.BlockSpec((DW_SLABS, CONV_K, V7X_LANES), lambda i: (0, 0, 0)),
            pl.BlockSpec((DW_SLABS, 1, V7X_LANES), lambda i: (0, 0, 0)),
            vec, vec,
            pl.BlockSpec((D_MODEL, D_MODEL), lambda i: (0, 0)),
            vec,
            pl.BlockSpec((TM_DW, D_MODEL), lambda i: (i, 0)),
            vec,
        ],
        out_specs=pl.BlockSpec((TM_DW, D_MODEL), lambda i: (i, 0)),
        out_shape=jax.ShapeDtypeStruct((t, D_MODEL), F32),
        scratch_shapes=[
            pltpu.VMEM((DW_SLABS, TM_DW + 2 * HALO, V7X_LANES), F32),
            pltpu.VMEM((DW_SLABS, TM_DW, V7X_LANES), F32),
            pltpu.VMEM((TM_DW, D_MODEL), BF16),
        ],
        compiler_params=_params(est, 1),
        name="conf_dw",
    )(u, u, u, w_dw, b_dw, ln_g, ln_b, w2, b2, h, gpost)


def _row(v):
    return v.reshape(1, -1).astype(F32)


def _encode(x, meta_tokens, p):
    batch, seq, _ = x.shape
    pad = (-(seq + N_META)) % CHUNK
    lp = seq + N_META + pad
    cps = lp // CHUNK
    meta = jnp.broadcast_to(meta_tokens[None].astype(x.dtype), (batch, N_META, D_MODEL))
    h = jnp.concatenate([jnp.zeros((batch, pad, D_MODEL), x.dtype), meta, x], axis=1)
    h = h.reshape(batch * lp, D_MODEL)

    half = RET_DK // 2
    inv = ROPE_BASE ** (-jnp.arange(half, dtype=F32) / half)
    ang = jnp.arange(lp).astype(F32)[:, None] * inv[None, :]
    cos = jnp.tile(jnp.cos(ang), (batch, 1))
    sin = jnp.tile(jnp.sin(ang), (batch, 1))

    for i in range(DEPTH):
        j = i // N_MIXERS
        if i % N_MIXERS == 0:
            proj = _proj(h, p["pre_mix"][i], p["ret_w_in"][j], cos, sin)
            gated = _ret_core(proj, p["ret_lg"][j], batch, lp)
            h = _outproj(gated, p["ret_w_out"][j], h, p["post_mix"][i])
        else:
            u = _glu(h, p["pre_mix"][i], p["conv_w_pw1"][j], p["conv_b_pw1"][j], cps)
            h = _dwconv(u, p["conv_w_dw"][j], p["conv_b_dw"][j], p["conv_ln_g"][j],
                        p["conv_ln_b"][j], p["conv_w_pw2"][j], p["conv_b_pw2"][j], h,
                        p["post_mix"][i], cps)
        h = _ffn(h, p["pre_ffn"][i], p["post_ffn"][i], p["ffn_w_up"][i], p["ffn_w_dw"][i],
                 p["ffn_b_dw"][i], p["ffn_w_down"][i])
    return h.reshape(batch, lp, D_MODEL)[:, pad + N_META:]


def kernel(x_prompt, x_sample, meta_tokens, norm_pre_mix, norm_post_mix, norm_pre_ffn, norm_post_ffn, ret_w_in, ret_decay_fwd, ret_decay_bwd, ret_w_out, conv_w_pw1, conv_b_pw1, conv_w_dw, conv_b_dw, conv_ln_g, conv_ln_b, conv_w_pw2, conv_b_pw2, ffn_w_up, ffn_w_dw, ffn_b_dw, ffn_w_down):
    fpad = FFN_PAD - FFN_HIDDEN
    w_up = ffn_w_up.astype(BF16)
    w_up = jnp.concatenate(
        [jnp.pad(w_up[:, :, :FFN_HIDDEN], ((0, 0), (0, 0), (0, fpad))),
         jnp.pad(w_up[:, :, FFN_HIDDEN:], ((0, 0), (0, 0), (0, fpad)))], axis=-1)
    p = {
        "pre_mix": [_row(v) for v in norm_pre_mix],
        "post_mix": [_row(v) for v in norm_post_mix],
        "pre_ffn": [_row(v) for v in norm_pre_ffn],
        "post_ffn": [_row(v) for v in norm_post_ffn],
        "ret_w_in": ret_w_in.astype(BF16),
        "ret_lg": jnp.stack([jax.nn.log_sigmoid(ret_decay_fwd.astype(F32)),
                             jax.nn.log_sigmoid(ret_decay_bwd.astype(F32))], axis=1),
        "ret_w_out": ret_w_out.astype(BF16),
        "conv_w_pw1": conv_w_pw1.astype(BF16),
        "conv_b_pw1": [_row(v) for v in conv_b_pw1],
        "conv_w_dw": conv_w_dw.astype(F32).reshape(-1, CONV_K, DW_SLABS, V7X_LANES)
                              .transpose(0, 2, 1, 3),
        "conv_b_dw": conv_b_dw.astype(F32).reshape(-1, DW_SLABS, 1, V7X_LANES),
        "conv_ln_g": [_row(v) for v in conv_ln_g],
        "conv_ln_b": [_row(v) for v in conv_ln_b],
        "conv_w_pw2": conv_w_pw2.astype(BF16),
        "conv_b_pw2": [_row(v) for v in conv_b_pw2],
        "ffn_w_up": w_up,
        "ffn_w_dw": jnp.pad(ffn_w_dw.astype(F32), ((0, 0), (0, 0), (0, fpad))),
        "ffn_b_dw": [_row(v) for v in jnp.pad(ffn_b_dw, ((0, 0), (0, fpad)))],
        "ffn_w_down": jnp.pad(ffn_w_down.astype(BF16), ((0, 0), (0, fpad), (0, 0))),
    }
    return (_encode(x_prompt, meta_tokens, p), _encode(x_sample, meta_tokens, p))
```

```python
import functools

import jax
import jax.numpy as jnp
from jax import lax
from jax.experimental import pallas as pl
from jax.experimental.pallas import tpu as pltpu

D_MODEL = 2048
DEPTH = 4
N_META = 16
N_MIXERS = 2
RET_HEADS = 8
RET_DK = D_MODEL // RET_HEADS
RET_DV = 2 * D_MODEL // RET_HEADS
HK = RET_HEADS * RET_DK
HV = RET_HEADS * RET_DV
CHUNK = 128
ROPE_BASE = 10000.0
CONV_K = 31
FFN_HIDDEN = 5504
FFN_CONV_K = 3
EPS = 1e-6

V7X_LANES = 128
V7X_VMEM_BYTES = 64 * 1024 * 1024
VMEM_LIMIT_CAP = 60000 * 1024

TM = 512
TM_DW = 512
HALO = 16
RET_STEP = 256
TN_PROJ = 2048
TK_OUT = 2048
TN_GLU = 1024
TF = 512
TD = 512
FFN_PAD = -(-FFN_HIDDEN // TF) * TF
NF = FFN_PAD // TF
DW_SLABS = D_MODEL // V7X_LANES
DW_ROWS = 64

F32 = jnp.float32
BF16 = jnp.bfloat16


def _params(est_bytes, n_axes):
    limit = min(VMEM_LIMIT_CAP, int(est_bytes * 1.25) + (4 << 20))
    return pltpu.CompilerParams(dimension_semantics=("arbitrary",) * n_axes,
                                vmem_limit_bytes=limit)


def _rms(x, g):
    return x * lax.rsqrt(jnp.mean(x * x, axis=-1, keepdims=True) + EPS) * g


def _keep_rows(tile, rows, chunks_per_seq):
    pad = CHUNK - N_META
    r = lax.broadcasted_iota(jnp.int32, (rows, 1), 0)
    sub = r // CHUNK
    in_pad = (r % CHUNK) < pad
    drop = jnp.zeros((rows, 1), jnp.bool_)
    for s in range(rows // CHUNK):
        first = lax.rem(tile * (rows // CHUNK) + s, chunks_per_seq) == 0
        drop = drop | ((sub == s) & in_pad & first)
    return jnp.where(drop, 0.0, 1.0).astype(F32)


def _proj_kernel(h_ref, g_ref, w_ref, cos_ref, sin_ref, o_ref, xn_ref):
    j = pl.program_id(1)

    @pl.when(j == 0)
    def _():
        xn_ref[...] = _rms(h_ref[...], g_ref[...]).astype(BF16)

    y = jnp.dot(xn_ref[...], w_ref[...], preferred_element_type=F32)
    n_rope = 2 * HK // TN_PROJ
    half = RET_DK // 2

    @pl.when(j < n_rope)
    def _():
        cos = cos_ref[...]
        sin = sin_ref[...]
        scale = jnp.where(j >= HK // TN_PROJ, RET_DK ** -0.5, 1.0).astype(F32)
        for hh in range(TN_PROJ // RET_DK):
            c0 = hh * RET_DK
            x1 = y[:, c0:c0 + half]
            x2 = y[:, c0 + half:c0 + RET_DK]
            o_ref[:, c0:c0 + half] = ((x1 * cos - x2 * sin) * scale).astype(BF16)
            o_ref[:, c0 + half:c0 + RET_DK] = ((x2 * cos + x1 * sin) * scale).astype(BF16)

    @pl.when(j >= n_rope)
    def _():
        o_ref[...] = y.astype(BF16)


def _proj(h, g, w, cos, sin):
    t = h.shape[0]
    n = w.shape[1]
    est = 2 * TM * D_MODEL * 4 + TM * D_MODEL * 2 + 2 * D_MODEL * TN_PROJ * 2 \
        + 2 * TM * TN_PROJ * 2 + 3 * TM * TN_PROJ * 4 + 4 * TM * V7X_LANES * 4
    return pl.pallas_call(
        _proj_kernel,
        grid=(t // TM, n // TN_PROJ),
        in_specs=[
            pl.BlockSpec((TM, D_MODEL), lambda i, j: (i, 0)),
            pl.BlockSpec((1, D_MODEL), lambda i, j: (0, 0)),
            pl.BlockSpec((D_MODEL, TN_PROJ), lambda i, j: (0, j)),
            pl.BlockSpec((TM, RET_DK // 2), lambda i, j: (i, 0)),
            pl.BlockSpec((TM, RET_DK // 2), lambda i, j: (i, 0)),
        ],
        out_specs=pl.BlockSpec((TM, TN_PROJ), lambda i, j: (i, j)),
        out_shape=jax.ShapeDtypeStruct((t, n), BF16),
        scratch_shapes=[pltpu.VMEM((TM, D_MODEL), BF16)],
        compiler_params=_params(est, 2),
        name="ret_proj",
    )(h, g, w, cos, sin)


def _ret_kernel(lg_ref, q_ref, k_ref, v_ref, g_ref, out_ref, o_ref, sf_ref, sb_ref, *, nc):
    hd = pl.program_id(1)
    lgf = lg_ref[0, hd]
    lgb = lg_ref[1, hd]
    contract_rows = (((0,), (0,)), ((), ()))
    contract_cols = (((1,), (1,)), ((), ()))
    one = jnp.ones((1, 1), F32)

    def tables(n):
        row = lax.broadcasted_iota(jnp.int32, (n, 1), 0).astype(F32)
        ii = lax.broadcasted_iota(jnp.int32, (n, n), 0)
        jj = lax.broadcasted_iota(jnp.int32, (n, n), 1)
        dist = (ii - jj).astype(F32)
        adist = jnp.abs(dist)
        return dict(
            dmat=jnp.where(dist >= 0, jnp.exp(adist * lgf), jnp.exp(adist * lgb)),
            sq_f=jnp.exp((row + 1.0) * lgf), sk_f=jnp.exp((n - 1.0 - row) * lgf),
            sq_b=jnp.exp((n - row) * lgb), sk_b=jnp.exp(row * lgb),
            cd_f=jnp.exp(one * (n * lgf)), cd_b=jnp.exp(one * (n * lgb)))

    def scaled(x, s):
        return (x.astype(F32) * s).astype(BF16)

    def emit(r, n, val, first):
        if first:
            o_ref[pl.ds(r, n), :] = val
        else:
            o_ref[pl.ds(r, n), :] += val

    def forward(r, n, tb, first):
        qc = q_ref[pl.ds(r, n), :]
        kc = k_ref[pl.ds(r, n), :]
        vc = v_ref[pl.ds(r, n), :]
        sc = lax.dot_general(qc, kc, contract_cols, preferred_element_type=F32) * tb["dmat"]
        sf = sf_ref[...]
        emit(r, n, jnp.dot(sc.astype(BF16), vc, preferred_element_type=F32)
             + jnp.dot(scaled(qc, tb["sq_f"]), sf.astype(BF16), preferred_element_type=F32),
             first)
        sf_ref[...] = sf * tb["cd_f"] + lax.dot_general(
            scaled(kc, tb["sk_f"]), vc, contract_rows, preferred_element_type=F32)

    def backward(r, n, tb, first):
        qc = q_ref[pl.ds(r, n), :]
        kc = k_ref[pl.ds(r, n), :]
        vc = v_ref[pl.ds(r, n), :]
        sb = sb_ref[...]
        emit(r, n, jnp.dot(scaled(qc, tb["sq_b"]), sb.astype(BF16),
                           preferred_element_type=F32), first)
        sb_ref[...] = sb * tb["cd_b"] + lax.dot_general(
            scaled(kc, tb["sk_b"]), vc, contract_rows, preferred_element_type=F32)

    sf_ref[...] = jnp.zeros_like(sf_ref)
    sb_ref[...] = jnp.zeros_like(sb_ref)

    n_steps = (nc * CHUNK - CHUNK) // RET_STEP
    forward(0, CHUNK, tables(CHUNK), True)
    tb = tables(RET_STEP)

    def step(t, first):
        forward(pl.multiple_of(CHUNK + t * RET_STEP, CHUNK), RET_STEP, tb, first)
        backward(pl.multiple_of(CHUNK + (n_steps - 1 - t) * RET_STEP, CHUNK), RET_STEP, tb,
                 first)

    def first_half(t, carry):
        step(t, True)
        return carry

    def second_half(t, carry):
        step(t, False)
        return carry

    lax.fori_loop(0, n_steps // 2, first_half, 0)
    lax.fori_loop(n_steps // 2, n_steps, second_half, 0)
    backward(0, CHUNK, tables(CHUNK), False)

    def finish(r, n):
        o = o_ref[pl.ds(r, n), :]
        mu = jnp.mean(o, axis=-1, keepdims=True)
        var = jnp.mean(jnp.square(o - mu), axis=-1, keepdims=True)
        on = (o - mu) * lax.rsqrt(var + EPS)
        gate = g_ref[pl.ds(r, n), :].astype(F32)
        out_ref[pl.ds(r, n), :] = (gate * jax.nn.sigmoid(gate) * on).astype(BF16)

    finish(0, CHUNK)

    def finish_step(t, carry):
        finish(pl.multiple_of(CHUNK + t * RET_STEP, CHUNK), RET_STEP)
        return carry

    lax.fori_loop(0, n_steps, finish_step, 0)


def _ret_core(proj, lg, batch, lp):
    t = proj.shape[0]
    assert (lp - CHUNK) % (2 * RET_STEP) == 0
    est = 2 * (2 * lp * RET_DK * 2 + 3 * lp * RET_DV * 2) + lp * RET_DV * 4 \
        + 2 * RET_DK * RET_DV * 4 + (8 << 20)
    return pl.pallas_call(
        functools.partial(_ret_kernel, nc=lp // CHUNK),
        grid=(batch, RET_HEADS),
        in_specs=[
            pl.BlockSpec(memory_space=pltpu.SMEM),
            pl.BlockSpec((lp, RET_DK), lambda b, h: (b, h)),
            pl.BlockSpec((lp, RET_DK), lambda b, h: (b, HK // RET_DK + h)),
            pl.BlockSpec((lp, RET_DV), lambda b, h: (b, 2 * HK // RET_DV + h)),
            pl.BlockSpec((lp, RET_DV), lambda b, h: (b, (2 * HK + HV) // RET_DV + h)),
        ],
        out_specs=pl.BlockSpec((lp, RET_DV), lambda b, h: (b, h)),
        out_shape=jax.ShapeDtypeStruct((t, HV), BF16),
        scratch_shapes=[pltpu.VMEM((lp, RET_DV), F32), pltpu.VMEM((RET_DK, RET_DV), F32),
                        pltpu.VMEM((RET_DK, RET_DV), F32)],
        compiler_params=_params(est, 2),
        name="ret_core",
    )(lg, proj, proj, proj, proj)


def _outproj_kernel(x_ref, w_ref, h_ref, g_ref, o_ref):
    k = pl.program_id(1)
    part = jnp.dot(x_ref[...], w_ref[...], preferred_element_type=F32)

    @pl.when(k == 0)
    def _():
        o_ref[...] = part

    @pl.when(k > 0)
    def _():
        o_ref[...] += part

    @pl.when(k == pl.num_programs(1) - 1)
    def _():
        o_ref[...] = h_ref[...] + _rms(o_ref[...], g_ref[...])


def _outproj(x, w, h, g):
    t, kdim = x.shape
    est = 2 * TM * TK_OUT * 2 + 2 * TK_OUT * D_MODEL * 2 + 6 * TM * D_MODEL * 4
    return pl.pallas_call(
        _outproj_kernel,
        grid=(t // TM, kdim // TK_OUT),
        in_specs=[
            pl.BlockSpec((TM, TK_OUT), lambda i, k: (i, k)),
            pl.BlockSpec((TK_OUT, D_MODEL), lambda i, k: (k, 0)),
            pl.BlockSpec((TM, D_MODEL), lambda i, k: (i, 0)),
            pl.BlockSpec((1, D_MODEL), lambda i, k: (0, 0)),
        ],
        out_specs=pl.BlockSpec((TM, D_MODEL), lambda i, k: (i, 0)),
        out_shape=jax.ShapeDtypeStruct((t, D_MODEL), F32),
        compiler_params=_params(est, 2),
        name="ret_out",
    )(x, w, h, g)


def _ffn_kernel(h_ref, hp_ref, hn_ref, gpre_ref, gpost_ref, wa_ref, wv_ref, wdw_ref,
                bdw_ref, wd_ref, o_ref, xn_ref, hm_ref):
    i = pl.program_id(0)
    j = pl.program_id(1)

    @pl.when(j == 0)
    def _():
        g = gpre_ref[...]
        xn_ref[0:HALO, :] = _rms(hp_ref[...], g).astype(BF16)
        xn_ref[HALO:HALO + TM, :] = _rms(h_ref[...], g).astype(BF16)
        nxt = jnp.where(i == pl.num_programs(0) - 1, 0.0, _rms(hn_ref[...], g))
        xn_ref[HALO + TM:, :] = nxt.astype(BF16)

    @pl.when(j < NF)
    def _():
        a = jnp.dot(xn_ref[...], wa_ref[...], preferred_element_type=F32)
        val = jnp.dot(xn_ref[HALO:HALO + TM, :], wv_ref[...], preferred_element_type=F32)
        w = wdw_ref[...]
        ac = (w[0:1, :] * a[HALO - 1:HALO - 1 + TM, :] + w[1:2, :] * a[HALO:HALO + TM, :]
              + w[2:3, :] * a[HALO + 1:HALO + 1 + TM, :] + bdw_ref[...])
        hm = (jax.nn.gelu(ac, approximate=True) * val).astype(BF16)
        hm_ref[:, pl.ds(pl.multiple_of(j * TF, TF), TF)] = hm

    @pl.when(j >= NF)
    def _():
        f = jnp.dot(hm_ref[...], wd_ref[...], preferred_element_type=F32)
        o_ref[:, pl.ds(pl.multiple_of((j - NF) * TD, TD), TD)] = f

    @pl.when(j == pl.num_programs(1) - 1)
    def _():
        o_ref[...] = h_ref[...] + _rms(o_ref[...], gpost_ref[...])


def _ffn(h, gpre, gpost, w_up, w_dw, b_dw, w_down):
    t = h.shape[0]
    nd = D_MODEL // TD
    nh = t // HALO
    ext = TM + 2 * HALO
    est = 4 * TM * D_MODEL * 4 + ext * D_MODEL * 2 + TM * FFN_PAD * 2 \
        + 4 * D_MODEL * TF * 2 + 2 * FFN_PAD * TD * 2 + 6 * ext * TF * 4 \
        + 2 * TM * D_MODEL * 4
    up_col = lambda i, j: (0, jnp.minimum(j, NF - 1))
    return pl.pallas_call(
        _ffn_kernel,
        grid=(t // TM, NF + nd),
        in_specs=[
            pl.BlockSpec((TM, D_MODEL), lambda i, j: (i, 0)),
            pl.BlockSpec((HALO, D_MODEL), lambda i, j: (jnp.maximum(i * (TM // HALO) - 1, 0), 0)),
            pl.BlockSpec((HALO, D_MODEL),
                         lambda i, j: (jnp.minimum((i + 1) * (TM // HALO), nh - 1), 0)),
            pl.BlockSpec((1, D_MODEL), lambda i, j: (0, 0)),
            pl.BlockSpec((1, D_MODEL), lambda i, j: (0, 0)),
            pl.BlockSpec((D_MODEL, TF), up_col),
            pl.BlockSpec((D_MODEL, TF), lambda i, j: (0, NF + jnp.minimum(j, NF - 1))),
            pl.BlockSpec((FFN_CONV_K, TF), up_col),
            pl.BlockSpec((1, TF), up_col),
            pl.BlockSpec((FFN_PAD, TD), lambda i, j: (0, jnp.clip(j - NF, 0, nd - 1))),
        ],
        out_specs=pl.BlockSpec((TM, D_MODEL), lambda i, j: (i, 0)),
        out_shape=jax.ShapeDtypeStruct((t, D_MODEL), F32),
        scratch_shapes=[pltpu.VMEM((ext, D_MODEL), BF16),
                        pltpu.VMEM((TM, FFN_PAD), BF16)],
        compiler_params=_params(est, 2),
        name="conv_ffn",
    )(h, h, h, gpre, gpost, w_up, w_up, w_dw, b_dw, w_down)


def _glu_kernel(h_ref, g_ref, wu_ref, wg_ref, bu_ref, bg_ref, o_ref, xn_ref, *, chunks_per_seq):
    i = pl.program_id(0)
    j = pl.program_id(1)

    @pl.when(j == 0)
    def _():
        xn_ref[...] = _rms(h_ref[...], g_ref[...]).astype(BF16)

    xn = xn_ref[...]
    u = jnp.dot(xn, wu_ref[...], preferred_element_type=F32) + bu_ref[...]
    gate = jnp.dot(xn, wg_ref[...], preferred_element_type=F32) + bg_ref[...]
    keep = _keep_rows(i, TM, chunks_per_seq)
    o_ref[...] = (u * jax.nn.sigmoid(gate) * keep).astype(BF16)


def _glu(h, g, w, b, chunks_per_seq):
    t = h.shape[0]
    nn = D_MODEL // TN_GLU
    est = 2 * TM * D_MODEL * 4 + TM * D_MODEL * 2 + 4 * D_MODEL * TN_GLU * 2 \
        + 2 * TM * TN_GLU * 2 + 4 * TM * TN_GLU * 4
    return pl.pallas_call(
        functools.partial(_glu_kernel, chunks_per_seq=chunks_per_seq),
        grid=(t // TM, nn),
        in_specs=[
            pl.BlockSpec((TM, D_MODEL), lambda i, j: (i, 0)),
            pl.BlockSpec((1, D_MODEL), lambda i, j: (0, 0)),
            pl.BlockSpec((D_MODEL, TN_GLU), lambda i, j: (0, j)),
            pl.BlockSpec((D_MODEL, TN_GLU), lambda i, j: (0, nn + j)),
            pl.BlockSpec((1, TN_GLU), lambda i, j: (0, j)),
            pl.BlockSpec((1, TN_GLU), lambda i, j: (0, nn + j)),
        ],
        out_specs=pl.BlockSpec((TM, TN_GLU), lambda i, j: (i, j)),
        out_shape=jax.ShapeDtypeStruct((t, D_MODEL), BF16),
        scratch_shapes=[pltpu.VMEM((TM, D_MODEL), BF16)],
        compiler_params=_params(est, 2),
        name="conf_glu",
    )(h, g, w, w, b, b)


def _dw_kernel(u_ref, up_ref, un_ref, wdw_ref, bdw_ref, lng_ref, lnb_ref, w2_ref, b2_ref,
               h_ref, gpost_ref, o_ref, ue_ref, cv_ref, y_ref, *, chunks_per_seq):
    i = pl.program_id(0)
    last = i == pl.num_programs(0) - 1
    for s in range(DW_SLABS):
        lanes = slice(s * V7X_LANES, (s + 1) * V7X_LANES)
        ue_ref[s, 0:HALO, :] = up_ref[:, lanes].astype(F32)
        ue_ref[s, HALO:HALO + TM_DW, :] = u_ref[:, lanes].astype(F32)
        ue_ref[s, HALO + TM_DW:, :] = jnp.where(last, 0.0, un_ref[:, lanes].astype(F32))

    shift = HALO - (CONV_K - 1) // 2

    def slab(s, carry):
        w = wdw_ref[s]
        bias = bdw_ref[s]
        for rb in range(TM_DW // DW_ROWS):
            r0 = rb * DW_ROWS
            acc = jnp.zeros((DW_ROWS, V7X_LANES), F32)
            for k in range(CONV_K):
                acc = acc + w[k:k + 1, :] * ue_ref[s, r0 + shift + k:r0 + shift + k + DW_ROWS, :]
            cv_ref[s, r0:r0 + DW_ROWS, :] = acc + bias
        return carry

    lax.fori_loop(0, DW_SLABS, slab, 0)

    for rb in range(TM_DW // DW_ROWS):
        rows = slice(rb * DW_ROWS, (rb + 1) * DW_ROWS)
        tot = cv_ref[0, rows, :]
        for s in range(1, DW_SLABS):
            tot = tot + cv_ref[s, rows, :]
        mu = jnp.sum(tot, axis=-1, keepdims=True) * (1.0 / D_MODEL)
        sq = jnp.square(cv_ref[0, rows, :] - mu)
        for s in range(1, DW_SLABS):
            sq = sq + jnp.square(cv_ref[s, rows, :] - mu)
        inv = lax.rsqrt(jnp.sum(sq, axis=-1, keepdims=True) * (1.0 / D_MODEL) + EPS)
        for s in range(DW_SLABS):
            lanes = slice(s * V7X_LANES, (s + 1) * V7X_LANES)
            ln = (cv_ref[s, rows, :] - mu) * inv * lng_ref[:, lanes] + lnb_ref[:, lanes]
            y_ref[rows, lanes] = (ln * jax.nn.sigmoid(ln)).astype(BF16)

    m = jnp.dot(y_ref[...], w2_ref[...], preferred_element_type=F32) + b2_ref[...]
    keep = _keep_rows(i, TM_DW, chunks_per_seq)
    o_ref[...] = h_ref[...] + _rms(m, gpost_ref[...]) * keep


def _dwconv(u, w_dw, b_dw, ln_g, ln_b, w2, b2, h, gpost, chunks_per_seq):
    t = u.shape[0]
    nh = t // HALO
    est = 2 * TM_DW * D_MODEL * 2 + (TM_DW + 2 * HALO) * D_MODEL * 4 + TM_DW * D_MODEL * 6 \
        + 2 * D_MODEL * D_MODEL * 2 + 4 * TM_DW * D_MODEL * 4 + 4 * TM_DW * D_MODEL * 4
    vec = pl.BlockSpec((1, D_MODEL), lambda i: (0, 0))
    return pl.pallas_call(
        functools.partial(_dw_kernel, chunks_per_seq=chunks_per_seq),
        grid=(t // TM_DW,),
        in_specs=[
            pl.BlockSpec((TM_DW, D_MODEL), lambda i: (i, 0)),
            pl.BlockSpec((HALO, D_MODEL), lambda i: (jnp.maximum(i * (TM_DW // HALO) - 1, 0), 0)),
            pl.BlockSpec((HALO, D_MODEL),
                         lambda i: (jnp.minimum((i + 1) * (TM_DW // HALO), nh - 1), 0)),
            pl.BlockSpec((DW_SLABS, CONV_K, V7X_LANES), lambda i: (0, 0, 0)),
            pl.BlockSpec((DW_SLABS, 1, V7X_LANES), lambda i: (0, 0, 0)),
            vec, vec,
            pl.BlockSpec((D_MODEL, D_MODEL), lambda i: (0, 0)),
            vec,
            pl.BlockSpec((TM_DW, D_MODEL), lambda i: (i, 0)),
            vec,
        ],
        out_specs=pl.BlockSpec((TM_DW, D_MODEL), lambda i: (i, 0)),
        out_shape=jax.ShapeDtypeStruct((t, D_MODEL), F32),
        scratch_shapes=[
            pltpu.VMEM((DW_SLABS, TM_DW + 2 * HALO, V7X_LANES), F32),
            pltpu.VMEM((DW_SLABS, TM_DW, V7X_LANES), F32),
            pltpu.VMEM((TM_DW, D_MODEL), BF16),
        ],
        compiler_params=_params(est, 1),
        name="conf_dw",
    )(u, u, u, w_dw, b_dw, ln_g, ln_b, w2, b2, h, gpost)


def _row(v):
    return v.reshape(1, -1).astype(F32)


def _encode(x, meta_tokens, p):
    batch, seq, _ = x.shape
    pad = (-(seq + N_META)) % CHUNK
    lp = seq + N_META + pad
    cps = lp // CHUNK
    meta = jnp.broadcast_to(meta_tokens[None].astype(x.dtype), (batch, N_META, D_MODEL))
    h = jnp.concatenate([jnp.zeros((batch, pad, D_MODEL), x.dtype), meta, x], axis=1)
    h = h.reshape(batch * lp, D_MODEL)

    half = RET_DK // 2
    inv = ROPE_BASE ** (-jnp.arange(half, dtype=F32) / half)
    ang = jnp.arange(lp).astype(F32)[:, None] * inv[None, :]
    cos = jnp.tile(jnp.cos(ang), (batch, 1))
    sin = jnp.tile(jnp.sin(ang), (batch, 1))

    for i in range(DEPTH):
        j = i // N_MIXERS
        if i % N_MIXERS == 0:
            proj = _proj(h, p["pre_mix"][i], p["ret_w_in"][j], cos, sin)
            gated = _ret_core(proj, p["ret_lg"][j], batch, lp)
            h = _outproj(gated, p["ret_w_out"][j], h, p["post_mix"][i])
        else:
            u = _glu(h, p["pre_mix"][i], p["conv_w_pw1"][j], p["conv_b_pw1"][j], cps)
            h = _dwconv(u, p["conv_w_dw"][j], p["conv_b_dw"][j], p["conv_ln_g"][j],
                        p["conv_ln_b"][j], p["conv_w_pw2"][j], p["conv_b_pw2"][j], h,
                        p["post_mix"][i], cps)
        h = _ffn(h, p["pre_ffn"][i], p["post_ffn"][i], p["ffn_w_up"][i], p["ffn_w_dw"][i],
                 p["ffn_b_dw"][i], p["ffn_w_down"][i])
    return h.reshape(batch, lp, D_MODEL)[:, pad + N_META:]


def kernel(x_prompt, x_sample, meta_tokens, norm_pre_mix, norm_post_mix, norm_pre_ffn, norm_post_ffn, ret_w_in, ret_decay_fwd, ret_decay_bwd, ret_w_out, conv_w_pw1, conv_b_pw1, conv_w_dw, conv_b_dw, conv_ln_g, conv_ln_b, conv_w_pw2, conv_b_pw2, ffn_w_up, ffn_w_dw, ffn_b_dw, ffn_w_down):
    fpad = FFN_PAD - FFN_HIDDEN
    w_up = jnp.concatenate(
        [jnp.pad(ffn_w_up[:, :, :FFN_HIDDEN], ((0, 0), (0, 0), (0, fpad))),
         jnp.pad(ffn_w_up[:, :, FFN_HIDDEN:], ((0, 0), (0, 0), (0, fpad)))],
        axis=-1).astype(BF16)
    p = {
        "pre_mix": [_row(v) for v in norm_pre_mix],
        "post_mix": [_row(v) for v in norm_post_mix],
        "pre_ffn": [_row(v) for v in norm_pre_ffn],
        "post_ffn": [_row(v) for v in norm_post_ffn],
        "ret_w_in": ret_w_in.astype(BF16),
        "ret_lg": jnp.stack([jax.nn.log_sigmoid(ret_decay_fwd.astype(F32)),
                             jax.nn.log_sigmoid(ret_decay_bwd.astype(F32))], axis=1),
        "ret_w_out": ret_w_out.astype(BF16),
        "conv_w_pw1": conv_w_pw1.astype(BF16),
        "conv_b_pw1": [_row(v) for v in conv_b_pw1],
        "conv_w_dw": conv_w_dw.astype(F32).reshape(-1, CONV_K, DW_SLABS, V7X_LANES)
                              .transpose(0, 2, 1, 3),
        "conv_b_dw": conv_b_dw.astype(F32).reshape(-1, DW_SLABS, 1, V7X_LANES),
        "conv_ln_g": [_row(v) for v in conv_ln_g],
        "conv_ln_b": [_row(v) for v in conv_ln_b],
        "conv_w_pw2": conv_w_pw2.astype(BF16),
        "conv_b_pw2": [_row(v) for v in conv_b_pw2],
        "ffn_w_up": w_up,
        "ffn_w_dw": jnp.pad(ffn_w_dw.astype(F32), ((0, 0), (0, 0), (0, fpad))),
        "ffn_b_dw": [_row(v) for v in jnp.pad(ffn_b_dw, ((0, 0), (0, fpad)))],
        "ffn_w_down": jnp.pad(ffn_w_down, ((0, 0), (0, fpad), (0, 0))).astype(BF16),
    }
    return (_encode(x_prompt, meta_tokens, p), _encode(x_sample, meta_tokens, p))
```

```python
import functools

import jax
import jax.numpy as jnp
from jax import lax
from jax.experimental import pallas as pl
from jax.experimental.pallas import tpu as pltpu

D_MODEL = 2048
DEPTH = 4
N_META = 16
N_MIXERS = 2
RET_HEADS = 8
RET_DK = D_MODEL // RET_HEADS
RET_DV = 2 * D_MODEL // RET_HEADS
HK = RET_HEADS * RET_DK
HV = RET_HEADS * RET_DV
CHUNK = 128
ROPE_BASE = 10000.0
CONV_K = 31
FFN_HIDDEN = 5504
FFN_CONV_K = 3
EPS = 1e-6

V7X_LANES = 128
V7X_VMEM_BYTES = 64 * 1024 * 1024
VMEM_LIMIT_CAP = 60000 * 1024

TM = 512
TM_DW = 512
HALO = 16
RET_STEP = 256
TN_PROJ = 2048
TK_OUT = 2048
TN_GLU = 1024
TF = 512
TD = 512
FFN_PAD = -(-FFN_HIDDEN // TF) * TF
NF = FFN_PAD // TF
DW_SLABS = D_MODEL // V7X_LANES
DW_ROWS = 64

F32 = jnp.float32
BF16 = jnp.bfloat16


def _params(est_bytes, n_axes):
    limit = min(VMEM_LIMIT_CAP, int(est_bytes * 1.25) + (4 << 20))
    return pltpu.CompilerParams(dimension_semantics=("arbitrary",) * n_axes,
                                vmem_limit_bytes=limit)


def _rms(x, g):
    return x * lax.rsqrt(jnp.mean(x * x, axis=-1, keepdims=True) + EPS) * g


def _keep_rows(tile, rows, chunks_per_seq):
    pad = CHUNK - N_META
    r = lax.broadcasted_iota(jnp.int32, (rows, 1), 0)
    sub = r // CHUNK
    in_pad = (r % CHUNK) < pad
    drop = jnp.zeros((rows, 1), jnp.bool_)
    for s in range(rows // CHUNK):
        first = lax.rem(tile * (rows // CHUNK) + s, chunks_per_seq) == 0
        drop = drop | ((sub == s) & in_pad & first)
    return jnp.where(drop, 0.0, 1.0).astype(F32)


def _proj_kernel(h_ref, g_ref, w_ref, cos_ref, sin_ref, o_ref, xn_ref):
    j = pl.program_id(1)

    @pl.when(j == 0)
    def _():
        xn_ref[...] = _rms(h_ref[...], g_ref[...]).astype(BF16)

    y = jnp.dot(xn_ref[...], w_ref[...], preferred_element_type=F32)
    n_rope = 2 * HK // TN_PROJ
    half = RET_DK // 2

    @pl.when(j < n_rope)
    def _():
        cos = cos_ref[...]
        sin = sin_ref[...]
        scale = jnp.where(j >= HK // TN_PROJ, RET_DK ** -0.5, 1.0).astype(F32)
        for hh in range(TN_PROJ // RET_DK):
            c0 = hh * RET_DK
            x1 = y[:, c0:c0 + half]
            x2 = y[:, c0 + half:c0 + RET_DK]
            o_ref[:, c0:c0 + half] = ((x1 * cos - x2 * sin) * scale).astype(BF16)
            o_ref[:, c0 + half:c0 + RET_DK] = ((x2 * cos + x1 * sin) * scale).astype(BF16)

    @pl.when(j >= n_rope)
    def _():
        o_ref[...] = y.astype(BF16)


def _proj(h, g, w, cos, sin):
    t = h.shape[0]
    n = w.shape[0] * TN_PROJ
    est = 2 * TM * D_MODEL * 4 + TM * D_MODEL * 2 + 2 * D_MODEL * TN_PROJ * 2 \
        + 2 * TM * TN_PROJ * 2 + 3 * TM * TN_PROJ * 4 + 4 * TM * V7X_LANES * 4
    return pl.pallas_call(
        _proj_kernel,
        grid=(t // TM, n // TN_PROJ),
        in_specs=[
            pl.BlockSpec((TM, D_MODEL), lambda i, j: (i, 0)),
            pl.BlockSpec((1, D_MODEL), lambda i, j: (0, 0)),
            pl.BlockSpec((None, D_MODEL, TN_PROJ), lambda i, j: (j, 0, 0)),
            pl.BlockSpec((TM, RET_DK // 2), lambda i, j: (i, 0)),
            pl.BlockSpec((TM, RET_DK // 2), lambda i, j: (i, 0)),
        ],
        out_specs=pl.BlockSpec((TM, TN_PROJ), lambda i, j: (i, j)),
        out_shape=jax.ShapeDtypeStruct((t, n), BF16),
        scratch_shapes=[pltpu.VMEM((TM, D_MODEL), BF16)],
        compiler_params=_params(est, 2),
        name="ret_proj",
    )(h, g, w, cos, sin)


def _ret_kernel(lg_ref, q_ref, k_ref, v_ref, g_ref, out_ref, o_ref, sf_ref, sb_ref, *, nc):
    hd = pl.program_id(1)
    lgf = lg_ref[0, hd]
    lgb = lg_ref[1, hd]
    contract_rows = (((0,), (0,)), ((), ()))
    contract_cols = (((1,), (1,)), ((), ()))
    one = jnp.ones((1, 1), F32)

    def tables(n):
        row = lax.broadcasted_iota(jnp.int32, (n, 1), 0).astype(F32)
        ii = lax.broadcasted_iota(jnp.int32, (n, n), 0)
        jj = lax.broadcasted_iota(jnp.int32, (n, n), 1)
        dist = (ii - jj).astype(F32)
        adist = jnp.abs(dist)
        return dict(
            dmat=jnp.where(dist >= 0, jnp.exp(adist * lgf), jnp.exp(adist * lgb)),
            sq_f=jnp.exp((row + 1.0) * lgf), sk_f=jnp.exp((n - 1.0 - row) * lgf),
            sq_b=jnp.exp((n - row) * lgb), sk_b=jnp.exp(row * lgb),
            cd_f=jnp.exp(one * (n * lgf)), cd_b=jnp.exp(one * (n * lgb)))

    def scaled(x, s):
        return (x.astype(F32) * s).astype(BF16)

    def emit(r, n, val, first):
        if first:
            o_ref[pl.ds(r, n), :] = val
        else:
            o_ref[pl.ds(r, n), :] += val

    def forward(r, n, tb, first):
        qc = q_ref[pl.ds(r, n), :]
        kc = k_ref[pl.ds(r, n), :]
        vc = v_ref[pl.ds(r, n), :]
        sc = lax.dot_general(qc, kc, contract_cols, preferred_element_type=F32) * tb["dmat"]
        sf = sf_ref[...]
        emit(r, n, jnp.dot(sc.astype(BF16), vc, preferred_element_type=F32)
             + jnp.dot(scaled(qc, tb["sq_f"]), sf.astype(BF16), preferred_element_type=F32),
             first)
        sf_ref[...] = sf * tb["cd_f"] + lax.dot_general(
            scaled(kc, tb["sk_f"]), vc, contract_rows, preferred_element_type=F32)

    def backward(r, n, tb, first):
        qc = q_ref[pl.ds(r, n), :]
        kc = k_ref[pl.ds(r, n), :]
        vc = v_ref[pl.ds(r, n), :]
        sb = sb_ref[...]
        emit(r, n, jnp.dot(scaled(qc, tb["sq_b"]), sb.astype(BF16),
                           preferred_element_type=F32), first)
        sb_ref[...] = sb * tb["cd_b"] + lax.dot_general(
            scaled(kc, tb["sk_b"]), vc, contract_rows, preferred_element_type=F32)

    sf_ref[...] = jnp.zeros_like(sf_ref)
    sb_ref[...] = jnp.zeros_like(sb_ref)

    n_steps = (nc * CHUNK - CHUNK) // RET_STEP
    forward(0, CHUNK, tables(CHUNK), True)
    tb = tables(RET_STEP)

    def step(t, first):
        forward(pl.multiple_of(CHUNK + t * RET_STEP, CHUNK), RET_STEP, tb, first)
        backward(pl.multiple_of(CHUNK + (n_steps - 1 - t) * RET_STEP, CHUNK), RET_STEP, tb,
                 first)

    def first_half(t, carry):
        step(t, True)
        return carry

    def second_half(t, carry):
        step(t, False)
        return carry

    lax.fori_loop(0, n_steps // 2, first_half, 0)
    lax.fori_loop(n_steps // 2, n_steps, second_half, 0)
    backward(0, CHUNK, tables(CHUNK), False)

    def finish(r, n):
        o = o_ref[pl.ds(r, n), :]
        mu = jnp.mean(o, axis=-1, keepdims=True)
        var = jnp.mean(jnp.square(o - mu), axis=-1, keepdims=True)
        on = (o - mu) * lax.rsqrt(var + EPS)
        gate = g_ref[pl.ds(r, n), :].astype(F32)
        out_ref[pl.ds(r, n), :] = (gate * jax.nn.sigmoid(gate) * on).astype(BF16)

    finish(0, CHUNK)

    def finish_step(t, carry):
        finish(pl.multiple_of(CHUNK + t * RET_STEP, CHUNK), RET_STEP)
        return carry

    lax.fori_loop(0, n_steps, finish_step, 0)


def _ret_core(proj, lg, batch, lp):
    t = proj.shape[0]
    assert (lp - CHUNK) % (2 * RET_STEP) == 0
    est = 2 * (2 * lp * RET_DK * 2 + 3 * lp * RET_DV * 2) + lp * RET_DV * 4 \
        + 2 * RET_DK * RET_DV * 4 + (8 << 20)
    return pl.pallas_call(
        functools.partial(_ret_kernel, nc=lp // CHUNK),
        grid=(batch, RET_HEADS),
        in_specs=[
            pl.BlockSpec(memory_space=pltpu.SMEM),
            pl.BlockSpec((lp, RET_DK), lambda b, h: (b, h)),
            pl.BlockSpec((lp, RET_DK), lambda b, h: (b, HK // RET_DK + h)),
            pl.BlockSpec((lp, RET_DV), lambda b, h: (b, 2 * HK // RET_DV + h)),
            pl.BlockSpec((lp, RET_DV), lambda b, h: (b, (2 * HK + HV) // RET_DV + h)),
        ],
        out_specs=pl.BlockSpec((lp, RET_DV), lambda b, h: (b, h)),
        out_shape=jax.ShapeDtypeStruct((t, HV), BF16),
        scratch_shapes=[pltpu.VMEM((lp, RET_DV), F32), pltpu.VMEM((RET_DK, RET_DV), F32),
                        pltpu.VMEM((RET_DK, RET_DV), F32)],
        compiler_params=_params(est, 2),
        name="ret_core",
    )(lg, proj, proj, proj, proj)


def _outproj_kernel(x_ref, w_ref, h_ref, g_ref, o_ref):
    k = pl.program_id(1)
    part = jnp.dot(x_ref[...], w_ref[...], preferred_element_type=F32)

    @pl.when(k == 0)
    def _():
        o_ref[...] = part

    @pl.when(k > 0)
    def _():
        o_ref[...] += part

    @pl.when(k == pl.num_programs(1) - 1)
    def _():
        o_ref[...] = h_ref[...] + _rms(o_ref[...], g_ref[...])


def _outproj(x, w, h, g):
    t, kdim = x.shape
    est = 2 * TM * TK_OUT * 2 + 2 * TK_OUT * D_MODEL * 2 + 6 * TM * D_MODEL * 4
    return pl.pallas_call(
        _outproj_kernel,
        grid=(t // TM, kdim // TK_OUT),
        in_specs=[
            pl.BlockSpec((TM, TK_OUT), lambda i, k: (i, k)),
            pl.BlockSpec((TK_OUT, D_MODEL), lambda i, k: (k, 0)),
            pl.BlockSpec((TM, D_MODEL), lambda i, k: (i, 0)),
            pl.BlockSpec((1, D_MODEL), lambda i, k: (0, 0)),
        ],
        out_specs=pl.BlockSpec((TM, D_MODEL), lambda i, k: (i, 0)),
        out_shape=jax.ShapeDtypeStruct((t, D_MODEL), F32),
        compiler_params=_params(est, 2),
        name="ret_out",
    )(x, w, h, g)


def _ffn_kernel(h_ref, hp_ref, hn_ref, gpre_ref, gpost_ref, wa_ref, wv_ref, wdw_ref,
                bdw_ref, wd_ref, o_ref, xn_ref, hm_ref):
    i = pl.program_id(0)
    j = pl.program_id(1)

    @pl.when(j == 0)
    def _():
        g = gpre_ref[...]
        xn_ref[0:HALO, :] = _rms(hp_ref[...], g).astype(BF16)
        xn_ref[HALO:HALO + TM, :] = _rms(h_ref[...], g).astype(BF16)
        nxt = jnp.where(i == pl.num_programs(0) - 1, 0.0, _rms(hn_ref[...], g))
        xn_ref[HALO + TM:, :] = nxt.astype(BF16)

    @pl.when(j < NF)
    def _():
        a = jnp.dot(xn_ref[...], wa_ref[...], preferred_element_type=F32)
        val = jnp.dot(xn_ref[HALO:HALO + TM, :], wv_ref[...], preferred_element_type=F32)
        w = wdw_ref[...]
        ac = (w[0:1, :] * a[HALO - 1:HALO - 1 + TM, :] + w[1:2, :] * a[HALO:HALO + TM, :]
              + w[2:3, :] * a[HALO + 1:HALO + 1 + TM, :] + bdw_ref[...])
        hm = (jax.nn.gelu(ac, approximate=True) * val).astype(BF16)
        hm_ref[:, pl.ds(pl.multiple_of(j * TF, TF), TF)] = hm

    @pl.when(j >= NF)
    def _():
        f = jnp.dot(hm_ref[...], wd_ref[...], preferred_element_type=F32)
        o_ref[:, pl.ds(pl.multiple_of((j - NF) * TD, TD), TD)] = f

    @pl.when(j == pl.num_programs(1) - 1)
    def _():
        o_ref[...] = h_ref[...] + _rms(o_ref[...], gpost_ref[...])


def _ffn(h, gpre, gpost, w_up, w_dw, b_dw, w_down):
    t = h.shape[0]
    nd = D_MODEL // TD
    nh = t // HALO
    ext = TM + 2 * HALO
    est = 4 * TM * D_MODEL * 4 + ext * D_MODEL * 2 + TM * FFN_PAD * 2 \
        + 4 * D_MODEL * TF * 2 + 2 * FFN_PAD * TD * 2 + 6 * ext * TF * 4 \
        + 2 * TM * D_MODEL * 4
    up_col = lambda i, j: (0, jnp.minimum(j, NF - 1))
    return pl.pallas_call(
        _ffn_kernel,
        grid=(t // TM, NF + nd),
        in_specs=[
            pl.BlockSpec((TM, D_MODEL), lambda i, j: (i, 0)),
            pl.BlockSpec((HALO, D_MODEL), lambda i, j: (jnp.maximum(i * (TM // HALO) - 1, 0), 0)),
            pl.BlockSpec((HALO, D_MODEL),
                         lambda i, j: (jnp.minimum((i + 1) * (TM // HALO), nh - 1), 0)),
            pl.BlockSpec((1, D_MODEL), lambda i, j: (0, 0)),
            pl.BlockSpec((1, D_MODEL), lambda i, j: (0, 0)),
            pl.BlockSpec((None, D_MODEL, TF), lambda i, j: (jnp.minimum(j, NF - 1), 0, 0)),
            pl.BlockSpec((None, D_MODEL, TF),
                         lambda i, j: (NF + jnp.minimum(j, NF - 1), 0, 0)),
            pl.BlockSpec((FFN_CONV_K, TF), up_col),
            pl.BlockSpec((1, TF), up_col),
            pl.BlockSpec((None, FFN_PAD, TD), lambda i, j: (jnp.clip(j - NF, 0, nd - 1), 0, 0)),
        ],
        out_specs=pl.BlockSpec((TM, D_MODEL), lambda i, j: (i, 0)),
        out_shape=jax.ShapeDtypeStruct((t, D_MODEL), F32),
        scratch_shapes=[pltpu.VMEM((ext, D_MODEL), BF16),
                        pltpu.VMEM((TM, FFN_PAD), BF16)],
        compiler_params=_params(est, 2),
        name="conv_ffn",
    )(h, h, h, gpre, gpost, w_up, w_up, w_dw, b_dw, w_down)


def _glu_kernel(h_ref, g_ref, wu_ref, wg_ref, bu_ref, bg_ref, o_ref, xn_ref, *, chunks_per_seq):
    i = pl.program_id(0)
    j = pl.program_id(1)

    @pl.when(j == 0)
    def _():
        xn_ref[...] = _rms(h_ref[...], g_ref[...]).astype(BF16)

    xn = xn_ref[...]
    u = jnp.dot(xn, wu_ref[...], preferred_element_type=F32) + bu_ref[...]
    gate = jnp.dot(xn, wg_ref[...], preferred_element_type=F32) + bg_ref[...]
    keep = _keep_rows(i, TM, chunks_per_seq)
    o_ref[...] = (u * jax.nn.sigmoid(gate) * keep).astype(BF16)


def _glu(h, g, w, b, chunks_per_seq):
    t = h.shape[0]
    nn = D_MODEL // TN_GLU
    est = 2 * TM * D_MODEL * 4 + TM * D_MODEL * 2 + 4 * D_MODEL * TN_GLU * 2 \
        + 2 * TM * TN_GLU * 2 + 4 * TM * TN_GLU * 4
    return pl.pallas_call(
        functools.partial(_glu_kernel, chunks_per_seq=chunks_per_seq),
        grid=(t // TM, nn),
        in_specs=[
            pl.BlockSpec((TM, D_MODEL), lambda i, j: (i, 0)),
            pl.BlockSpec((1, D_MODEL), lambda i, j: (0, 0)),
            pl.BlockSpec((None, D_MODEL, TN_GLU), lambda i, j: (j, 0, 0)),
            pl.BlockSpec((None, D_MODEL, TN_GLU), lambda i, j: (nn + j, 0, 0)),
            pl.BlockSpec((1, TN_GLU), lambda i, j: (0, j)),
            pl.BlockSpec((1, TN_GLU), lambda i, j: (0, nn + j)),
        ],
        out_specs=pl.BlockSpec((TM, TN_GLU), lambda i, j: (i, j)),
        out_shape=jax.ShapeDtypeStruct((t, D_MODEL), BF16),
        scratch_shapes=[pltpu.VMEM((TM, D_MODEL), BF16)],
        compiler_params=_params(est, 2),
        name="conf_glu",
    )(h, g, w, w, b, b)


def _dw_kernel(u_ref, up_ref, un_ref, wdw_ref, bdw_ref, lng_ref, lnb_ref, w2_ref, b2_ref,
               h_ref, gpost_ref, o_ref, ue_ref, cv_ref, y_ref, *, chunks_per_seq):
    i = pl.program_id(0)
    last = i == pl.num_programs(0) - 1
    for s in range(DW_SLABS):
        lanes = slice(s * V7X_LANES, (s + 1) * V7X_LANES)
        ue_ref[s, 0:HALO, :] = up_ref[:, lanes].astype(F32)
        ue_ref[s, HALO:HALO + TM_DW, :] = u_ref[:, lanes].astype(F32)
        ue_ref[s, HALO + TM_DW:, :] = jnp.where(last, 0.0, un_ref[:, lanes].astype(F32))

    shift = HALO - (CONV_K - 1) // 2

    def slab(s, carry):
        w = wdw_ref[s]
        bias = bdw_ref[s]
        for rb in range(TM_DW // DW_ROWS):
            r0 = rb * DW_ROWS
            acc = jnp.zeros((DW_ROWS, V7X_LANES), F32)
            for k in range(CONV_K):
                acc = acc + w[k:k + 1, :] * ue_ref[s, r0 + shift + k:r0 + shift + k + DW_ROWS, :]
            cv_ref[s, r0:r0 + DW_ROWS, :] = acc + bias
        return carry

    lax.fori_loop(0, DW_SLABS, slab, 0)

    for rb in range(TM_DW // DW_ROWS):
        rows = slice(rb * DW_ROWS, (rb + 1) * DW_ROWS)
        tot = cv_ref[0, rows, :]
        for s in range(1, DW_SLABS):
            tot = tot + cv_ref[s, rows, :]
        mu = jnp.sum(tot, axis=-1, keepdims=True) * (1.0 / D_MODEL)
        sq = jnp.square(cv_ref[0, rows, :] - mu)
        for s in range(1, DW_SLABS):
            sq = sq + jnp.square(cv_ref[s, rows, :] - mu)
        inv = lax.rsqrt(jnp.sum(sq, axis=-1, keepdims=True) * (1.0 / D_MODEL) + EPS)
        for s in range(DW_SLABS):
            lanes = slice(s * V7X_LANES, (s + 1) * V7X_LANES)
            ln = (cv_ref[s, rows, :] - mu) * inv * lng_ref[:, lanes] + lnb_ref[:, lanes]
            y_ref[rows, lanes] = (ln * jax.nn.sigmoid(ln)).astype(BF16)

    m = jnp.dot(y_ref[...], w2_ref[...], preferred_element_type=F32) + b2_ref[...]
    keep = _keep_rows(i, TM_DW, chunks_per_seq)
    o_ref[...] = h_ref[...] + _rms(m, gpost_ref[...]) * keep


def _dwconv(u, w_dw, b_dw, ln_g, ln_b, w2, b2, h, gpost, chunks_per_seq):
    t = u.shape[0]
    nh = t // HALO
    est = 2 * TM_DW * D_MODEL * 2 + (TM_DW + 2 * HALO) * D_MODEL * 4 + TM_DW * D_MODEL * 6 \
        + 2 * D_MODEL * D_MODEL * 2 + 4 * TM_DW * D_MODEL * 4 + 4 * TM_DW * D_MODEL * 4
    vec = pl.BlockSpec((1, D_MODEL), lambda i: (0, 0))
    return pl.pallas_call(
        functools.partial(_dw_kernel, chunks_per_seq=chunks_per_seq),
        grid=(t // TM_DW,),
        in_specs=[
            pl.BlockSpec((TM_DW, D_MODEL), lambda i: (i, 0)),
            pl.BlockSpec((HALO, D_MODEL), lambda i: (jnp.maximum(i * (TM_DW // HALO) - 1, 0), 0)),
            pl.BlockSpec((HALO, D_MODEL),
                         lambda i: (jnp.minimum((i + 1) * (TM_DW // HALO), nh - 1), 0)),
            pl.BlockSpec((DW_SLABS, CONV_K, V7X_LANES), lambda i: (0, 0, 0)),
            pl.BlockSpec((DW_SLABS, 1, V7X_LANES), lambda i: (0, 0, 0)),
            vec, vec,
            pl.BlockSpec((D_MODEL, D_MODEL), lambda i: (0, 0)),
            vec,
            pl.BlockSpec((TM_DW, D_MODEL), lambda i: (i, 0)),
            vec,
        ],
        out_specs=pl.BlockSpec((TM_DW, D_MODEL), lambda i: (i, 0)),
        out_shape=jax.ShapeDtypeStruct((t, D_MODEL), F32),
        scratch_shapes=[
            pltpu.VMEM((DW_SLABS, TM_DW + 2 * HALO, V7X_LANES), F32),
            pltpu.VMEM((DW_SLABS, TM_DW, V7X_LANES), F32),
            pltpu.VMEM((TM_DW, D_MODEL), BF16),
        ],
        compiler_params=_params(est, 1),
        name="conf_dw",
    )(u, u, u, w_dw, b_dw, ln_g, ln_b, w2, b2, h, gpost)


def _row(v):
    return v.reshape(1, -1).astype(F32)


def _col_blocks(w, tn):
    layers, k, n = w.shape
    return w.reshape(layers, k, n // tn, tn).transpose(0, 2, 1, 3).astype(BF16)


def _encode(x, meta_tokens, p):
    batch, seq, _ = x.shape
    pad = (-(seq + N_META)) % CHUNK
    lp = seq + N_META + pad
    cps = lp // CHUNK
    meta = jnp.broadcast_to(meta_tokens[None].astype(x.dtype), (batch, N_META, D_MODEL))
    h = jnp.concatenate([jnp.zeros((batch, pad, D_MODEL), x.dtype), meta, x], axis=1)
    h = h.reshape(batch * lp, D_MODEL)

    half = RET_DK // 2
    inv = ROPE_BASE ** (-jnp.arange(half, dtype=F32) / half)
    ang = jnp.arange(lp).astype(F32)[:, None] * inv[None, :]
    cos = jnp.tile(jnp.cos(ang), (batch, 1))
    sin = jnp.tile(jnp.sin(ang), (batch, 1))

    for i in range(DEPTH):
        j = i // N_MIXERS
        if i % N_MIXERS == 0:
            proj = _proj(h, p["pre_mix"][i], p["ret_w_in"][j], cos, sin)
            gated = _ret_core(proj, p["ret_lg"][j], batch, lp)
            h = _outproj(gated, p["ret_w_out"][j], h, p["post_mix"][i])
        else:
            u = _glu(h, p["pre_mix"][i], p["conv_w_pw1"][j], p["conv_b_pw1"][j], cps)
            h = _dwconv(u, p["conv_w_dw"][j], p["conv_b_dw"][j], p["conv_ln_g"][j],
                        p["conv_ln_b"][j], p["conv_w_pw2"][j], p["conv_b_pw2"][j], h,
                        p["post_mix"][i], cps)
        h = _ffn(h, p["pre_ffn"][i], p["post_ffn"][i], p["ffn_w_up"][i], p["ffn_w_dw"][i],
                 p["ffn_b_dw"][i], p["ffn_w_down"][i])
    return h.reshape(batch, lp, D_MODEL)[:, pad + N_META:]


def kernel(x_prompt, x_sample, meta_tokens, norm_pre_mix, norm_post_mix, norm_pre_ffn, norm_post_ffn, ret_w_in, ret_decay_fwd, ret_decay_bwd, ret_w_out, conv_w_pw1, conv_b_pw1, conv_w_dw, conv_b_dw, conv_ln_g, conv_ln_b, conv_w_pw2, conv_b_pw2, ffn_w_up, ffn_w_dw, ffn_b_dw, ffn_w_down):
    fpad = FFN_PAD - FFN_HIDDEN
    w_up = jnp.concatenate(
        [jnp.pad(ffn_w_up[:, :, :FFN_HIDDEN], ((0, 0), (0, 0), (0, fpad))),
         jnp.pad(ffn_w_up[:, :, FFN_HIDDEN:], ((0, 0), (0, 0), (0, fpad)))],
        axis=-1)
    p = {
        "pre_mix": [_row(v) for v in norm_pre_mix],
        "post_mix": [_row(v) for v in norm_post_mix],
        "pre_ffn": [_row(v) for v in norm_pre_ffn],
        "post_ffn": [_row(v) for v in norm_post_ffn],
        "ret_w_in": _col_blocks(ret_w_in, TN_PROJ),
        "ret_lg": jnp.stack([jax.nn.log_sigmoid(ret_decay_fwd.astype(F32)),
                             jax.nn.log_sigmoid(ret_decay_bwd.astype(F32))], axis=1),
        "ret_w_out": ret_w_out.astype(BF16),
        "conv_w_pw1": _col_blocks(conv_w_pw1, TN_GLU),
        "conv_b_pw1": [_row(v) for v in conv_b_pw1],
        "conv_w_dw": conv_w_dw.astype(F32).reshape(-1, CONV_K, DW_SLABS, V7X_LANES)
                              .transpose(0, 2, 1, 3),
        "conv_b_dw": conv_b_dw.astype(F32).reshape(-1, DW_SLABS, 1, V7X_LANES),
        "conv_ln_g": [_row(v) for v in conv_ln_g],
        "conv_ln_b": [_row(v) for v in conv_ln_b],
        "conv_w_pw2": conv_w_pw2.astype(BF16),
        "conv_b_pw2": [_row(v) for v in conv_b_pw2],
        "ffn_w_up": _col_blocks(w_up, TF),
        "ffn_w_dw": jnp.pad(ffn_w_dw.astype(F32), ((0, 0), (0, 0), (0, fpad))),
        "ffn_b_dw": [_row(v) for v in jnp.pad(ffn_b_dw, ((0, 0), (0, fpad)))],
        "ffn_w_down": _col_blocks(jnp.pad(ffn_w_down, ((0, 0), (0, fpad), (0, 0))), TD),
    }
    return (_encode(x_prompt, meta_tokens, p), _encode(x_sample, meta_tokens, p))
```

```python
import functools

import jax
import jax.numpy as jnp
from jax import lax
from jax.experimental import pallas as pl
from jax.experimental.pallas import tpu as pltpu

D_MODEL = 2048
DEPTH = 4
N_META = 16
N_MIXERS = 2
RET_HEADS = 8
RET_DK = D_MODEL // RET_HEADS
RET_DV = 2 * D_MODEL // RET_HEADS
HK = RET_HEADS * RET_DK
HV = RET_HEADS * RET_DV
CHUNK = 128
ROPE_BASE = 10000.0
CONV_K = 31
FFN_HIDDEN = 5504
FFN_CONV_K = 3
EPS = 1e-6

V7X_LANES = 128
V7X_VMEM_BYTES = 64 * 1024 * 1024
VMEM_LIMIT_CAP = 60000 * 1024

TM = 512
TM_DW = 512
HALO = 16
RET_STEP = 256
TN_PROJ = 2048
TK_OUT = 2048
TN_GLU = 1024
TF = 512
TD = 512
FFN_PAD = -(-FFN_HIDDEN // TF) * TF
NF = FFN_PAD // TF
DW_SLABS = D_MODEL // V7X_LANES
DW_ROWS = 64

F32 = jnp.float32
BF16 = jnp.bfloat16


def _params(est_bytes, n_axes):
    limit = min(VMEM_LIMIT_CAP, int(est_bytes * 1.25) + (4 << 20))
    return pltpu.CompilerParams(dimension_semantics=("arbitrary",) * n_axes,
                                vmem_limit_bytes=limit)


def _rms(x, g):
    return x * lax.rsqrt(jnp.mean(x * x, axis=-1, keepdims=True) + EPS) * g


def _keep_rows(tile, rows, chunks_per_seq):
    pad = CHUNK - N_META
    r = lax.broadcasted_iota(jnp.int32, (rows, 1), 0)
    sub = r // CHUNK
    in_pad = (r % CHUNK) < pad
    drop = jnp.zeros((rows, 1), jnp.bool_)
    for s in range(rows // CHUNK):
        first = lax.rem(tile * (rows // CHUNK) + s, chunks_per_seq) == 0
        drop = drop | ((sub == s) & in_pad & first)
    return jnp.where(drop, 0.0, 1.0).astype(F32)


def _proj_kernel(h_ref, g_ref, w_ref, cos_ref, sin_ref, o_ref, xn_ref):
    j = pl.program_id(1)

    @pl.when(j == 0)
    def _():
        xn_ref[...] = _rms(h_ref[...], g_ref[...]).astype(BF16)

    y = jnp.dot(xn_ref[...], w_ref[...], preferred_element_type=F32)
    n_rope = 2 * HK // TN_PROJ
    half = RET_DK // 2

    @pl.when(j < n_rope)
    def _():
        cos = cos_ref[...]
        sin = sin_ref[...]
        scale = jnp.where(j >= HK // TN_PROJ, RET_DK ** -0.5, 1.0).astype(F32)
        for hh in range(TN_PROJ // RET_DK):
            c0 = hh * RET_DK
            x1 = y[:, c0:c0 + half]
            x2 = y[:, c0 + half:c0 + RET_DK]
            o_ref[:, c0:c0 + half] = ((x1 * cos - x2 * sin) * scale).astype(BF16)
            o_ref[:, c0 + half:c0 + RET_DK] = ((x2 * cos + x1 * sin) * scale).astype(BF16)

    @pl.when(j >= n_rope)
    def _():
        o_ref[...] = y.astype(BF16)


def _proj(h, g, w, cos, sin):
    t = h.shape[0]
    n = w.shape[1]
    est = 2 * TM * D_MODEL * 4 + TM * D_MODEL * 2 + 2 * D_MODEL * TN_PROJ * 2 \
        + 2 * TM * TN_PROJ * 2 + 3 * TM * TN_PROJ * 4 + 4 * TM * V7X_LANES * 4
    return pl.pallas_call(
        _proj_kernel,
        grid=(t // TM, n // TN_PROJ),
        in_specs=[
            pl.BlockSpec((TM, D_MODEL), lambda i, j: (i, 0)),
            pl.BlockSpec((1, D_MODEL), lambda i, j: (0, 0)),
            pl.BlockSpec((D_MODEL, TN_PROJ), lambda i, j: (0, j)),
            pl.BlockSpec((TM, RET_DK // 2), lambda i, j: (i, 0)),
            pl.BlockSpec((TM, RET_DK // 2), lambda i, j: (i, 0)),
        ],
        out_specs=pl.BlockSpec((TM, TN_PROJ), lambda i, j: (i, j)),
        out_shape=jax.ShapeDtypeStruct((t, n), BF16),
        scratch_shapes=[pltpu.VMEM((TM, D_MODEL), BF16)],
        compiler_params=_params(est, 2),
        name="ret_proj",
    )(h, g, w, cos, sin)


def _ret_kernel(lg_ref, q_ref, k_ref, v_ref, g_ref, out_ref, o_ref, sf_ref, sb_ref, *, nc):
    hd = pl.program_id(1)
    lgf = lg_ref[0, hd]
    lgb = lg_ref[1, hd]
    contract_rows = (((0,), (0,)), ((), ()))
    contract_cols = (((1,), (1,)), ((), ()))
    one = jnp.ones((1, 1), F32)

    def tables(n):
        row = lax.broadcasted_iota(jnp.int32, (n, 1), 0).astype(F32)
        ii = lax.broadcasted_iota(jnp.int32, (n, n), 0)
        jj = lax.broadcasted_iota(jnp.int32, (n, n), 1)
        dist = (ii - jj).astype(F32)
        adist = jnp.abs(dist)
        return dict(
            dmat=jnp.where(dist >= 0, jnp.exp(adist * lgf), jnp.exp(adist * lgb)),
            sq_f=jnp.exp((row + 1.0) * lgf), sk_f=jnp.exp((n - 1.0 - row) * lgf),
            sq_b=jnp.exp((n - row) * lgb), sk_b=jnp.exp(row * lgb),
            cd_f=jnp.exp(one * (n * lgf)), cd_b=jnp.exp(one * (n * lgb)))

    def scaled(x, s):
        return (x.astype(F32) * s).astype(BF16)

    def emit(r, n, val, first):
        if first:
            o_ref[pl.ds(r, n), :] = val
        else:
            o_ref[pl.ds(r, n), :] += val

    def forward(r, n, tb, first):
        qc = q_ref[pl.ds(r, n), :]
        kc = k_ref[pl.ds(r, n), :]
        vc = v_ref[pl.ds(r, n), :]
        sc = lax.dot_general(qc, kc, contract_cols, preferred_element_type=F32) * tb["dmat"]
        sf = sf_ref[...]
        emit(r, n, jnp.dot(sc.astype(BF16), vc, preferred_element_type=F32)
             + jnp.dot(scaled(qc, tb["sq_f"]), sf.astype(BF16), preferred_element_type=F32),
             first)
        sf_ref[...] = sf * tb["cd_f"] + lax.dot_general(
            scaled(kc, tb["sk_f"]), vc, contract_rows, preferred_element_type=F32)

    def backward(r, n, tb, first):
        qc = q_ref[pl.ds(r, n), :]
        kc = k_ref[pl.ds(r, n), :]
        vc = v_ref[pl.ds(r, n), :]
        sb = sb_ref[...]
        emit(r, n, jnp.dot(scaled(qc, tb["sq_b"]), sb.astype(BF16),
                           preferred_element_type=F32), first)
        sb_ref[...] = sb * tb["cd_b"] + lax.dot_general(
            scaled(kc, tb["sk_b"]), vc, contract_rows, preferred_element_type=F32)

    sf_ref[...] = jnp.zeros_like(sf_ref)
    sb_ref[...] = jnp.zeros_like(sb_ref)

    n_steps = (nc * CHUNK - CHUNK) // RET_STEP
    forward(0, CHUNK, tables(CHUNK), True)
    tb = tables(RET_STEP)

    def chunk_row(c):
        r = CHUNK + c * RET_STEP
        return r if isinstance(r, int) else pl.multiple_of(r, CHUNK)

    def step(t, first):
        forward(chunk_row(t), RET_STEP, tb, first)
        backward(chunk_row(n_steps - 1 - t), RET_STEP, tb, first)

    def finish(r, n):
        o = o_ref[pl.ds(r, n), :]
        mu = jnp.mean(o, axis=-1, keepdims=True)
        var = jnp.mean(jnp.square(o - mu), axis=-1, keepdims=True)
        on = (o - mu) * lax.rsqrt(var + EPS)
        gate = g_ref[pl.ds(r, n), :].astype(F32)
        out_ref[pl.ds(r, n), :] = (gate * jax.nn.sigmoid(gate) * on).astype(BF16)

    def finish_chunk(c):
        finish(chunk_row(c), RET_STEP)

    def first_half(t, carry):
        step(t, True)
        return carry

    def second_half(t, carry):
        finish_chunk(t - 1)
        finish_chunk(n_steps - t)
        step(t, False)
        return carry

    half = n_steps // 2
    lax.fori_loop(0, half, first_half, 0)
    step(half, False)
    lax.fori_loop(half + 1, n_steps, second_half, 0)
    finish_chunk(n_steps - 1)
    finish_chunk(0)
    backward(0, CHUNK, tables(CHUNK), False)
    finish(0, CHUNK)


def _ret_core(proj, lg, batch, lp):
    t = proj.shape[0]
    assert (lp - CHUNK) % (2 * RET_STEP) == 0
    est = 2 * (2 * lp * RET_DK * 2 + 3 * lp * RET_DV * 2) + lp * RET_DV * 4 \
        + 2 * RET_DK * RET_DV * 4 + (8 << 20)
    return pl.pallas_call(
        functools.partial(_ret_kernel, nc=lp // CHUNK),
        grid=(batch, RET_HEADS),
        in_specs=[
            pl.BlockSpec(memory_space=pltpu.SMEM),
            pl.BlockSpec((lp, RET_DK), lambda b, h: (b, h)),
            pl.BlockSpec((lp, RET_DK), lambda b, h: (b, HK // RET_DK + h)),
            pl.BlockSpec((lp, RET_DV), lambda b, h: (b, 2 * HK // RET_DV + h)),
            pl.BlockSpec((lp, RET_DV), lambda b, h: (b, (2 * HK + HV) // RET_DV + h)),
        ],
        out_specs=pl.BlockSpec((lp, RET_DV), lambda b, h: (b, h)),
        out_shape=jax.ShapeDtypeStruct((t, HV), BF16),
        scratch_shapes=[pltpu.VMEM((lp, RET_DV), F32), pltpu.VMEM((RET_DK, RET_DV), F32),
                        pltpu.VMEM((RET_DK, RET_DV), F32)],
        compiler_params=_params(est, 2),
        name="ret_core",
    )(lg, proj, proj, proj, proj)


def _outproj_kernel(x_ref, w_ref, h_ref, g_ref, o_ref):
    k = pl.program_id(1)
    part = jnp.dot(x_ref[...], w_ref[...], preferred_element_type=F32)

    @pl.when(k == 0)
    def _():
        o_ref[...] = part

    @pl.when(k > 0)
    def _():
        o_ref[...] += part

    @pl.when(k == pl.num_programs(1) - 1)
    def _():
        o_ref[...] = h_ref[...] + _rms(o_ref[...], g_ref[...])


def _outproj(x, w, h, g):
    t, kdim = x.shape
    est = 2 * TM * TK_OUT * 2 + 2 * TK_OUT * D_MODEL * 2 + 6 * TM * D_MODEL * 4
    return pl.pallas_call(
        _outproj_kernel,
        grid=(t // TM, kdim // TK_OUT),
        in_specs=[
            pl.BlockSpec((TM, TK_OUT), lambda i, k: (i, k)),
            pl.BlockSpec((TK_OUT, D_MODEL), lambda i, k: (k, 0)),
            pl.BlockSpec((TM, D_MODEL), lambda i, k: (i, 0)),
            pl.BlockSpec((1, D_MODEL), lambda i, k: (0, 0)),
        ],
        out_specs=pl.BlockSpec((TM, D_MODEL), lambda i, k: (i, 0)),
        out_shape=jax.ShapeDtypeStruct((t, D_MODEL), F32),
        compiler_params=_params(est, 2),
        name="ret_out",
    )(x, w, h, g)


def _ffn_kernel(h_ref, hp_ref, hn_ref, gpre_ref, gpost_ref, wa_ref, wv_ref, wdw_ref,
                bdw_ref, wd_ref, o_ref, xn_ref, hm_ref, ss_ref):
    i = pl.program_id(0)
    j = pl.program_id(1)

    @pl.when(j == 0)
    def _():
        g = gpre_ref[...]
        xn_ref[0:HALO, :] = _rms(hp_ref[...], g).astype(BF16)
        xn_ref[HALO:HALO + TM, :] = _rms(h_ref[...], g).astype(BF16)
        nxt = jnp.where(i == pl.num_programs(0) - 1, 0.0, _rms(hn_ref[...], g))
        xn_ref[HALO + TM:, :] = nxt.astype(BF16)
        ss_ref[...] = jnp.zeros_like(ss_ref)

    @pl.when(j < NF)
    def _():
        a = jnp.dot(xn_ref[...], wa_ref[...], preferred_element_type=F32)
        val = jnp.dot(xn_ref[HALO:HALO + TM, :], wv_ref[...], preferred_element_type=F32)
        w = wdw_ref[...]
        ac = (w[0:1, :] * a[HALO - 1:HALO - 1 + TM, :] + w[1:2, :] * a[HALO:HALO + TM, :]
              + w[2:3, :] * a[HALO + 1:HALO + 1 + TM, :] + bdw_ref[...])
        hm = (jax.nn.gelu(ac, approximate=True) * val).astype(BF16)
        hm_ref[:, pl.ds(pl.multiple_of(j * TF, TF), TF)] = hm

    @pl.when(j >= NF)
    def _():
        f = jnp.dot(hm_ref[...], wd_ref[...], preferred_element_type=F32)
        o_ref[:, pl.ds(pl.multiple_of((j - NF) * TD, TD), TD)] = f
        ss_ref[...] += jnp.sum(f * f, axis=-1, keepdims=True)

    @pl.when(j == pl.num_programs(1) - 1)
    def _():
        inv = lax.rsqrt(ss_ref[...] * (1.0 / D_MODEL) + EPS)
        o_ref[...] = h_ref[...] + o_ref[...] * inv * gpost_ref[...]


def _ffn(h, gpre, gpost, w_up, w_dw, b_dw, w_down):
    t = h.shape[0]
    nd = D_MODEL // TD
    nh = t // HALO
    ext = TM + 2 * HALO
    est = 4 * TM * D_MODEL * 4 + ext * D_MODEL * 2 + TM * FFN_PAD * 2 \
        + 4 * D_MODEL * TF * 2 + 2 * FFN_PAD * TD * 2 + 6 * ext * TF * 4 \
        + 2 * TM * D_MODEL * 4
    up_col = lambda i, j: (0, jnp.minimum(j, NF - 1))
    return pl.pallas_call(
        _ffn_kernel,
        grid=(t // TM, NF + nd),
        in_specs=[
            pl.BlockSpec((TM, D_MODEL), lambda i, j: (i, 0)),
            pl.BlockSpec((HALO, D_MODEL), lambda i, j: (jnp.maximum(i * (TM // HALO) - 1, 0), 0)),
            pl.BlockSpec((HALO, D_MODEL),
                         lambda i, j: (jnp.minimum((i + 1) * (TM // HALO), nh - 1), 0)),
            pl.BlockSpec((1, D_MODEL), lambda i, j: (0, 0)),
            pl.BlockSpec((1, D_MODEL), lambda i, j: (0, 0)),
            pl.BlockSpec((D_MODEL, TF), up_col),
            pl.BlockSpec((D_MODEL, TF), lambda i, j: (0, NF + jnp.minimum(j, NF - 1))),
            pl.BlockSpec((FFN_CONV_K, TF), up_col),
            pl.BlockSpec((1, TF), up_col),
            pl.BlockSpec((FFN_PAD, TD), lambda i, j: (0, jnp.clip(j - NF, 0, nd - 1))),
        ],
        out_specs=pl.BlockSpec((TM, D_MODEL), lambda i, j: (i, 0)),
        out_shape=jax.ShapeDtypeStruct((t, D_MODEL), F32),
        scratch_shapes=[pltpu.VMEM((ext, D_MODEL), BF16),
                        pltpu.VMEM((TM, FFN_PAD), BF16),
                        pltpu.VMEM((TM, 1), F32)],
        compiler_params=_params(est, 2),
        name="conv_ffn",
    )(h, h, h, gpre, gpost, w_up, w_up, w_dw, b_dw, w_down)


def _glu_kernel(h_ref, g_ref, wu_ref, wg_ref, bu_ref, bg_ref, o_ref, xn_ref, *, chunks_per_seq):
    i = pl.program_id(0)
    j = pl.program_id(1)

    @pl.when(j == 0)
    def _():
        xn_ref[...] = _rms(h_ref[...], g_ref[...]).astype(BF16)

    xn = xn_ref[...]
    u = jnp.dot(xn, wu_ref[...], preferred_element_type=F32) + bu_ref[...]
    gate = jnp.dot(xn, wg_ref[...], preferred_element_type=F32) + bg_ref[...]
    keep = _keep_rows(i, TM, chunks_per_seq)
    o_ref[...] = (u * jax.nn.sigmoid(gate) * keep).astype(BF16)


def _glu(h, g, w, b, chunks_per_seq):
    t = h.shape[0]
    nn = D_MODEL // TN_GLU
    est = 2 * TM * D_MODEL * 4 + TM * D_MODEL * 2 + 4 * D_MODEL * TN_GLU * 2 \
        + 2 * TM * TN_GLU * 2 + 4 * TM * TN_GLU * 4
    return pl.pallas_call(
        functools.partial(_glu_kernel, chunks_per_seq=chunks_per_seq),
        grid=(t // TM, nn),
        in_specs=[
            pl.BlockSpec((TM, D_MODEL), lambda i, j: (i, 0)),
            pl.BlockSpec((1, D_MODEL), lambda i, j: (0, 0)),
            pl.BlockSpec((D_MODEL, TN_GLU), lambda i, j: (0, j)),
            pl.BlockSpec((D_MODEL, TN_GLU), lambda i, j: (0, nn + j)),
            pl.BlockSpec((1, TN_GLU), lambda i, j: (0, j)),
            pl.BlockSpec((1, TN_GLU), lambda i, j: (0, nn + j)),
        ],
        out_specs=pl.BlockSpec((TM, TN_GLU), lambda i, j: (i, j)),
        out_shape=jax.ShapeDtypeStruct((t, D_MODEL), BF16),
        scratch_shapes=[pltpu.VMEM((TM, D_MODEL), BF16)],
        compiler_params=_params(est, 2),
        name="conf_glu",
    )(h, g, w, w, b, b)


def _dw_kernel(u_ref, up_ref, un_ref, wdw_ref, bdw_ref, lng_ref, lnb_ref, w2_ref, b2_ref,
               h_ref, gpost_ref, o_ref, ue_ref, cv_ref, y_ref, *, chunks_per_seq):
    i = pl.program_id(0)
    last = i == pl.num_programs(0) - 1
    for s in range(DW_SLABS):
        lanes = slice(s * V7X_LANES, (s + 1) * V7X_LANES)
        ue_ref[s, 0:HALO, :] = up_ref[:, lanes].astype(F32)
        ue_ref[s, HALO:HALO + TM_DW, :] = u_ref[:, lanes].astype(F32)
        ue_ref[s, HALO + TM_DW:, :] = jnp.where(last, 0.0, un_ref[:, lanes].astype(F32))

    shift = HALO - (CONV_K - 1) // 2

    def slab(s, carry):
        w = wdw_ref[s]
        bias = bdw_ref[s]
        for rb in range(TM_DW // DW_ROWS):
            r0 = rb * DW_ROWS
            acc = jnp.zeros((DW_ROWS, V7X_LANES), F32)
            for k in range(CONV_K):
                acc = acc + w[k:k + 1, :] * ue_ref[s, r0 + shift + k:r0 + shift + k + DW_ROWS, :]
            cv_ref[s, r0:r0 + DW_ROWS, :] = acc + bias
        return carry

    lax.fori_loop(0, DW_SLABS, slab, 0)

    for rb in range(TM_DW // DW_ROWS):
        rows = slice(rb * DW_ROWS, (rb + 1) * DW_ROWS)
        tot = cv_ref[0, rows, :]
        for s in range(1, DW_SLABS):
            tot = tot + cv_ref[s, rows, :]
        mu = jnp.sum(tot, axis=-1, keepdims=True) * (1.0 / D_MODEL)
        sq = jnp.square(cv_ref[0, rows, :] - mu)
        for s in range(1, DW_SLABS):
            sq = sq + jnp.square(cv_ref[s, rows, :] - mu)
        inv = lax.rsqrt(jnp.sum(sq, axis=-1, keepdims=True) * (1.0 / D_MODEL) + EPS)
        for s in range(DW_SLABS):
            lanes = slice(s * V7X_LANES, (s + 1) * V7X_LANES)
            ln = (cv_ref[s, rows, :] - mu) * inv * lng_ref[:, lanes] + lnb_ref[:, lanes]
            y_ref[rows, lanes] = (ln * jax.nn.sigmoid(ln)).astype(BF16)

    m = jnp.dot(y_ref[...], w2_ref[...], preferred_element_type=F32) + b2_ref[...]
    keep = _keep_rows(i, TM_DW, chunks_per_seq)
    o_ref[...] = h_ref[...] + _rms(m, gpost_ref[...]) * keep


def _dwconv(u, w_dw, b_dw, ln_g, ln_b, w2, b2, h, gpost, chunks_per_seq):
    t = u.shape[0]
    nh = t // HALO
    est = 2 * TM_DW * D_MODEL * 2 + (TM_DW + 2 * HALO) * D_MODEL * 4 + TM_DW * D_MODEL * 6 \
        + 2 * D_MODEL * D_MODEL * 2 + 4 * TM_DW * D_MODEL * 4 + 4 * TM_DW * D_MODEL * 4
    vec = pl.BlockSpec((1, D_MODEL), lambda i: (0, 0))
    return pl.pallas_call(
        functools.partial(_dw_kernel, chunks_per_seq=chunks_per_seq),
        grid=(t // TM_DW,),
        in_specs=[
            pl.BlockSpec((TM_DW, D_MODEL), lambda i: (i, 0)),
            pl.BlockSpec((HALO, D_MODEL), lambda i: (jnp.maximum(i * (TM_DW // HALO) - 1, 0), 0)),
            pl.BlockSpec((HALO, D_MODEL),
                         lambda i: (jnp.minimum((i + 1) * (TM_DW // HALO), nh - 1), 0)),
            pl.BlockSpec((DW_SLABS, CONV_K, V7X_LANES), lambda i: (0, 0, 0)),
            pl.BlockSpec((DW_SLABS, 1, V7X_LANES), lambda i: (0, 0, 0)),
            vec, vec,
            pl.BlockSpec((D_MODEL, D_MODEL), lambda i: (0, 0)),
            vec,
            pl.BlockSpec((TM_DW, D_MODEL), lambda i: (i, 0)),
            vec,
        ],
        out_specs=pl.BlockSpec((TM_DW, D_MODEL), lambda i: (i, 0)),
        out_shape=jax.ShapeDtypeStruct((t, D_MODEL), F32),
        scratch_shapes=[
            pltpu.VMEM((DW_SLABS, TM_DW + 2 * HALO, V7X_LANES), F32),
            pltpu.VMEM((DW_SLABS, TM_DW, V7X_LANES), F32),
            pltpu.VMEM((TM_DW, D_MODEL), BF16),
        ],
        compiler_params=_params(est, 1),
        name="conf_dw",
    )(u, u, u, w_dw, b_dw, ln_g, ln_b, w2, b2, h, gpost)


def _row(v):
    return v.reshape(1, -1).astype(F32)


def _encode(x, meta_tokens, p):
    batch, seq, _ = x.shape
    pad = (-(seq + N_META)) % CHUNK
    lp = seq + N_META + pad
    cps = lp // CHUNK
    meta = jnp.broadcast_to(meta_tokens[None].astype(x.dtype), (batch, N_META, D_MODEL))
    h = jnp.concatenate([jnp.zeros((batch, pad, D_MODEL), x.dtype), meta, x], axis=1)
    h = h.reshape(batch * lp, D_MODEL)

    half = RET_DK // 2
    inv = ROPE_BASE ** (-jnp.arange(half, dtype=F32) / half)
    ang = jnp.arange(lp).astype(F32)[:, None] * inv[None, :]
    cos = jnp.tile(jnp.cos(ang), (batch, 1))
    sin = jnp.tile(jnp.sin(ang), (batch, 1))

    for i in range(DEPTH):
        j = i // N_MIXERS
        if i % N_MIXERS == 0:
            proj = _proj(h, p["pre_mix"][i], p["ret_w_in"][j], cos, sin)
            gated = _ret_core(proj, p["ret_lg"][j], batch, lp)
            h = _outproj(gated, p["ret_w_out"][j], h, p["post_mix"][i])
        else:
            u = _glu(h, p["pre_mix"][i], p["conv_w_pw1"][j], p["conv_b_pw1"][j], cps)
            h = _dwconv(u, p["conv_w_dw"][j], p["conv_b_dw"][j], p["conv_ln_g"][j],
                        p["conv_ln_b"][j], p["conv_w_pw2"][j], p["conv_b_pw2"][j], h,
                        p["post_mix"][i], cps)
        h = _ffn(h, p["pre_ffn"][i], p["post_ffn"][i], p["ffn_w_up"][i], p["ffn_w_dw"][i],
                 p["ffn_b_dw"][i], p["ffn_w_down"][i])
    return h.reshape(batch, lp, D_MODEL)[:, pad + N_META:]


def kernel(x_prompt, x_sample, meta_tokens, norm_pre_mix, norm_post_mix, norm_pre_ffn, norm_post_ffn, ret_w_in, ret_decay_fwd, ret_decay_bwd, ret_w_out, conv_w_pw1, conv_b_pw1, conv_w_dw, conv_b_dw, conv_ln_g, conv_ln_b, conv_w_pw2, conv_b_pw2, ffn_w_up, ffn_w_dw, ffn_b_dw, ffn_w_down):
    fpad = FFN_PAD - FFN_HIDDEN
    w_up = jnp.concatenate(
        [jnp.pad(ffn_w_up[:, :, :FFN_HIDDEN], ((0, 0), (0, 0), (0, fpad))),
         jnp.pad(ffn_w_up[:, :, FFN_HIDDEN:], ((0, 0), (0, 0), (0, fpad)))],
        axis=-1).astype(BF16)
    p = {
        "pre_mix": [_row(v) for v in norm_pre_mix],
        "post_mix": [_row(v) for v in norm_post_mix],
        "pre_ffn": [_row(v) for v in norm_pre_ffn],
        "post_ffn": [_row(v) for v in norm_post_ffn],
        "ret_w_in": ret_w_in.astype(BF16),
        "ret_lg": jnp.stack([jax.nn.log_sigmoid(ret_decay_fwd.astype(F32)),
                             jax.nn.log_sigmoid(ret_decay_bwd.astype(F32))], axis=1),
        "ret_w_out": ret_w_out.astype(BF16),
        "conv_w_pw1": conv_w_pw1.astype(BF16),
        "conv_b_pw1": [_row(v) for v in conv_b_pw1],
        "conv_w_dw": conv_w_dw.astype(F32).reshape(-1, CONV_K, DW_SLABS, V7X_LANES)
                              .transpose(0, 2, 1, 3),
        "conv_b_dw": conv_b_dw.astype(F32).reshape(-1, DW_SLABS, 1, V7X_LANES),
        "conv_ln_g": [_row(v) for v in conv_ln_g],
        "conv_ln_b": [_row(v) for v in conv_ln_b],
        "conv_w_pw2": conv_w_pw2.astype(BF16),
        "conv_b_pw2": [_row(v) for v in conv_b_pw2],
        "ffn_w_up": w_up,
        "ffn_w_dw": jnp.pad(ffn_w_dw.astype(F32), ((0, 0), (0, 0), (0, fpad))),
        "ffn_b_dw": [_row(v) for v in jnp.pad(ffn_b_dw, ((0, 0), (0, fpad)))],
        "ffn_w_down": jnp.pad(ffn_w_down, ((0, 0), (0, fpad), (0, 0))).astype(BF16),
    }
    return (_encode(x_prompt, meta_tokens, p), _encode(x_sample, meta_tokens, p))
```

```python
import functools

import jax
import jax.numpy as jnp
from jax import lax
from jax.experimental import pallas as pl
from jax.experimental.pallas import tpu as pltpu

D_MODEL = 2048
DEPTH = 4
N_META = 16
N_MIXERS = 2
RET_HEADS = 8
RET_DK = D_MODEL // RET_HEADS
RET_DV = 2 * D_MODEL // RET_HEADS
HK = RET_HEADS * RET_DK
HV = RET_HEADS * RET_DV
CHUNK = 128
ROPE_BASE = 10000.0
CONV_K = 31
FFN_HIDDEN = 5504
FFN_CONV_K = 3
EPS = 1e-6

V7X_LANES = 128
V7X_VMEM_BYTES = 64 * 1024 * 1024
VMEM_LIMIT_CAP = 60000 * 1024

TM = 512
TM_DW = 512
HALO = 16
RET_STEP = 256
TN_PROJ = 2048
TK_OUT = 2048
TN_GLU = 1024
TF = 512
TD = 512
FFN_PAD = -(-FFN_HIDDEN // TF) * TF
NF = FFN_PAD // TF
DW_SLABS = D_MODEL // V7X_LANES
DW_ROWS = 64

F32 = jnp.float32
BF16 = jnp.bfloat16


def _params(est_bytes, n_axes):
    limit = min(VMEM_LIMIT_CAP, int(est_bytes * 1.25) + (4 << 20))
    return pltpu.CompilerParams(dimension_semantics=("arbitrary",) * n_axes,
                                vmem_limit_bytes=limit)


def _rms(x, g):
    return x * lax.rsqrt(jnp.mean(x * x, axis=-1, keepdims=True) + EPS) * g


def _keep_rows(tile, rows, chunks_per_seq):
    pad = CHUNK - N_META
    r = lax.broadcasted_iota(jnp.int32, (rows, 1), 0)
    sub = r // CHUNK
    in_pad = (r % CHUNK) < pad
    drop = jnp.zeros((rows, 1), jnp.bool_)
    for s in range(rows // CHUNK):
        first = lax.rem(tile * (rows // CHUNK) + s, chunks_per_seq) == 0
        drop = drop | ((sub == s) & in_pad & first)
    return jnp.where(drop, 0.0, 1.0).astype(F32)


def _proj_kernel(h_ref, g_ref, w_ref, cos_ref, sin_ref, o_ref, xn_ref):
    j = pl.program_id(1)

    @pl.when(j == 0)
    def _():
        xn_ref[...] = _rms(h_ref[...], g_ref[...]).astype(BF16)

    y = jnp.dot(xn_ref[...], w_ref[...], preferred_element_type=F32)
    n_rope = 2 * HK // TN_PROJ
    half = RET_DK // 2

    @pl.when(j < n_rope)
    def _():
        cos = cos_ref[...]
        sin = sin_ref[...]
        scale = jnp.where(j >= HK // TN_PROJ, RET_DK ** -0.5, 1.0).astype(F32)
        for hh in range(TN_PROJ // RET_DK):
            c0 = hh * RET_DK
            x1 = y[:, c0:c0 + half]
            x2 = y[:, c0 + half:c0 + RET_DK]
            o_ref[:, c0:c0 + half] = ((x1 * cos - x2 * sin) * scale).astype(BF16)
            o_ref[:, c0 + half:c0 + RET_DK] = ((x2 * cos + x1 * sin) * scale).astype(BF16)

    @pl.when(j >= n_rope)
    def _():
        o_ref[...] = y.astype(BF16)


def _proj(h, g, w, cos, sin):
    t = h.shape[0]
    n = w.shape[1]
    est = 2 * TM * D_MODEL * 4 + TM * D_MODEL * 2 + 2 * D_MODEL * TN_PROJ * 2 \
        + 2 * TM * TN_PROJ * 2 + 3 * TM * TN_PROJ * 4 + 4 * TM * V7X_LANES * 4
    return pl.pallas_call(
        _proj_kernel,
        grid=(t // TM, n // TN_PROJ),
        in_specs=[
            pl.BlockSpec((TM, D_MODEL), lambda i, j: (i, 0)),
            pl.BlockSpec((1, D_MODEL), lambda i, j: (0, 0)),
            pl.BlockSpec((D_MODEL, TN_PROJ), lambda i, j: (0, j)),
            pl.BlockSpec((TM, RET_DK // 2), lambda i, j: (i, 0)),
            pl.BlockSpec((TM, RET_DK // 2), lambda i, j: (i, 0)),
        ],
        out_specs=pl.BlockSpec((TM, TN_PROJ), lambda i, j: (i, j)),
        out_shape=jax.ShapeDtypeStruct((t, n), BF16),
        scratch_shapes=[pltpu.VMEM((TM, D_MODEL), BF16)],
        compiler_params=_params(est, 2),
        name="ret_proj",
    )(h, g, w, cos, sin)


def _ret_kernel(lg_ref, q_ref, k_ref, v_ref, g_ref, out_ref, o_ref, sf_ref, sb_ref, *, nc):
    hd = pl.program_id(1)
    lgf = lg_ref[0, hd]
    lgb = lg_ref[1, hd]
    contract_rows = (((0,), (0,)), ((), ()))
    contract_cols = (((1,), (1,)), ((), ()))
    one = jnp.ones((1, 1), F32)

    def tables(n):
        row = lax.broadcasted_iota(jnp.int32, (n, 1), 0).astype(F32)
        ii = lax.broadcasted_iota(jnp.int32, (n, n), 0)
        jj = lax.broadcasted_iota(jnp.int32, (n, n), 1)
        dist = (ii - jj).astype(F32)
        adist = jnp.abs(dist)
        return dict(
            dmat=jnp.where(dist >= 0, jnp.exp(adist * lgf), jnp.exp(adist * lgb)),
            sq_f=jnp.exp((row + 1.0) * lgf), sk_f=jnp.exp((n - 1.0 - row) * lgf),
            sq_b=jnp.exp((n - row) * lgb), sk_b=jnp.exp(row * lgb),
            cd_f=jnp.exp(one * (n * lgf)), cd_b=jnp.exp(one * (n * lgb)))

    def scaled(x, s):
        return (x.astype(F32) * s).astype(BF16)

    def emit(r, n, val, first):
        if first:
            o_ref[pl.ds(r, n), :] = val
        else:
            o_ref[pl.ds(r, n), :] += val

    def forward(r, n, tb, first):
        qc = q_ref[pl.ds(r, n), :]
        kc = k_ref[pl.ds(r, n), :]
        vc = v_ref[pl.ds(r, n), :]
        sc = lax.dot_general(qc, kc, contract_cols, preferred_element_type=F32) * tb["dmat"]
        sf = sf_ref[...]
        emit(r, n, jnp.dot(sc.astype(BF16), vc, preferred_element_type=F32)
             + jnp.dot(scaled(qc, tb["sq_f"]), sf.astype(BF16), preferred_element_type=F32),
             first)
        sf_ref[...] = sf * tb["cd_f"] + lax.dot_general(
            scaled(kc, tb["sk_f"]), vc, contract_rows, preferred_element_type=F32)

    def backward(r, n, tb, first):
        qc = q_ref[pl.ds(r, n), :]
        kc = k_ref[pl.ds(r, n), :]
        vc = v_ref[pl.ds(r, n), :]
        sb = sb_ref[...]
        emit(r, n, jnp.dot(scaled(qc, tb["sq_b"]), sb.astype(BF16),
                           preferred_element_type=F32), first)
        sb_ref[...] = sb * tb["cd_b"] + lax.dot_general(
            scaled(kc, tb["sk_b"]), vc, contract_rows, preferred_element_type=F32)

    sf_ref[...] = jnp.zeros_like(sf_ref)
    sb_ref[...] = jnp.zeros_like(sb_ref)

    n_steps = (nc * CHUNK - CHUNK) // RET_STEP
    forward(0, CHUNK, tables(CHUNK), True)
    tb = tables(RET_STEP)

    def chunk_row(c):
        r = CHUNK + c * RET_STEP
        return r if isinstance(r, int) else pl.multiple_of(r, CHUNK)

    def step(t, first):
        forward(chunk_row(t), RET_STEP, tb, first)
        backward(chunk_row(n_steps - 1 - t), RET_STEP, tb, first)

    def finish(r, n):
        o = o_ref[pl.ds(r, n), :]
        mu = jnp.mean(o, axis=-1, keepdims=True)
        var = jnp.mean(jnp.square(o - mu), axis=-1, keepdims=True)
        on = (o - mu) * lax.rsqrt(var + EPS)
        gate = g_ref[pl.ds(r, n), :].astype(F32)
        out_ref[pl.ds(r, n), :] = (gate * jax.nn.sigmoid(gate) * on).astype(BF16)

    def finish_chunk(c):
        finish(chunk_row(c), RET_STEP)

    def first_half(t, carry):
        step(t, True)
        return carry

    def second_half(t, carry):
        finish_chunk(t - 1)
        finish_chunk(n_steps - t)
        step(t, False)
        return carry

    half = n_steps // 2
    lax.fori_loop(0, half, first_half, 0)
    step(half, False)
    lax.fori_loop(half + 1, n_steps, second_half, 0)
    finish_chunk(n_steps - 1)
    finish_chunk(0)
    backward(0, CHUNK, tables(CHUNK), False)
    finish(0, CHUNK)


def _ret_core(proj, lg, batch, lp):
    t = proj.shape[0]
    assert (lp - CHUNK) % (2 * RET_STEP) == 0
    est = 2 * (2 * lp * RET_DK * 2 + 3 * lp * RET_DV * 2) + lp * RET_DV * 4 \
        + 2 * RET_DK * RET_DV * 4 + (8 << 20)
    return pl.pallas_call(
        functools.partial(_ret_kernel, nc=lp // CHUNK),
        grid=(batch, RET_HEADS),
        in_specs=[
            pl.BlockSpec(memory_space=pltpu.SMEM),
            pl.BlockSpec((lp, RET_DK), lambda b, h: (b, h)),
            pl.BlockSpec((lp, RET_DK), lambda b, h: (b, HK // RET_DK + h)),
            pl.BlockSpec((lp, RET_DV), lambda b, h: (b, 2 * HK // RET_DV + h)),
            pl.BlockSpec((lp, RET_DV), lambda b, h: (b, (2 * HK + HV) // RET_DV + h)),
        ],
        out_specs=pl.BlockSpec((lp, RET_DV), lambda b, h: (b, h)),
        out_shape=jax.ShapeDtypeStruct((t, HV), BF16),
        scratch_shapes=[pltpu.VMEM((lp, RET_DV), F32), pltpu.VMEM((RET_DK, RET_DV), F32),
                        pltpu.VMEM((RET_DK, RET_DV), F32)],
        compiler_params=_params(est, 2),
        name="ret_core",
    )(lg, proj, proj, proj, proj)


def _outproj_kernel(x_ref, w_ref, h_ref, g_ref, o_ref):
    k = pl.program_id(1)
    part = jnp.dot(x_ref[...], w_ref[...], preferred_element_type=F32)

    @pl.when(k == 0)
    def _():
        o_ref[...] = part

    @pl.when(k > 0)
    def _():
        o_ref[...] += part

    @pl.when(k == pl.num_programs(1) - 1)
    def _():
        o_ref[...] = h_ref[...] + _rms(o_ref[...], g_ref[...])


def _outproj(x, w, h, g):
    t, kdim = x.shape
    est = 2 * TM * TK_OUT * 2 + 2 * TK_OUT * D_MODEL * 2 + 6 * TM * D_MODEL * 4
    return pl.pallas_call(
        _outproj_kernel,
        grid=(t // TM, kdim // TK_OUT),
        in_specs=[
            pl.BlockSpec((TM, TK_OUT), lambda i, k: (i, k)),
            pl.BlockSpec((TK_OUT, D_MODEL), lambda i, k: (k, 0)),
            pl.BlockSpec((TM, D_MODEL), lambda i, k: (i, 0)),
            pl.BlockSpec((1, D_MODEL), lambda i, k: (0, 0)),
        ],
        out_specs=pl.BlockSpec((TM, D_MODEL), lambda i, k: (i, 0)),
        out_shape=jax.ShapeDtypeStruct((t, D_MODEL), F32),
        compiler_params=_params(est, 2),
        name="ret_out",
    )(x, w, h, g)


def _ffn_kernel(h_ref, hp_ref, hn_ref, gpre_ref, gpost_ref, wa_ref, wv_ref, wdw_ref,
                bdw_ref, wd_ref, o_ref, xn_ref, hm_ref):
    i = pl.program_id(0)
    j = pl.program_id(1)

    @pl.when(j == 0)
    def _():
        g = gpre_ref[...]
        xn_ref[0:HALO, :] = _rms(hp_ref[...], g).astype(BF16)
        xn_ref[HALO:HALO + TM, :] = _rms(h_ref[...], g).astype(BF16)
        nxt = jnp.where(i == pl.num_programs(0) - 1, 0.0, _rms(hn_ref[...], g))
        xn_ref[HALO + TM:, :] = nxt.astype(BF16)

    @pl.when(j < NF)
    def _():
        a = jnp.dot(xn_ref[...], wa_ref[...], preferred_element_type=F32)
        val = jnp.dot(xn_ref[HALO:HALO + TM, :], wv_ref[...], preferred_element_type=F32)
        w = wdw_ref[...]
        ac = (w[0:1, :] * a[HALO - 1:HALO - 1 + TM, :] + w[1:2, :] * a[HALO:HALO + TM, :]
              + w[2:3, :] * a[HALO + 1:HALO + 1 + TM, :] + bdw_ref[...])
        hm = (jax.nn.gelu(ac, approximate=True) * val).astype(BF16)
        hm_ref[:, pl.ds(pl.multiple_of(j * TF, TF), TF)] = hm

    @pl.when(j >= NF)
    def _():
        f = jnp.dot(hm_ref[...], wd_ref[...], preferred_element_type=F32)
        o_ref[:, pl.ds(pl.multiple_of((j - NF) * TD, TD), TD)] = f

    @pl.when(j == pl.num_programs(1) - 1)
    def _():
        o_ref[...] = h_ref[...] + _rms(o_ref[...], gpost_ref[...])


def _ffn(h, gpre, gpost, w_up, w_dw, b_dw, w_down):
    t = h.shape[0]
    nd = D_MODEL // TD
    nh = t // HALO
    ext = TM + 2 * HALO
    est = 4 * TM * D_MODEL * 4 + ext * D_MODEL * 2 + TM * FFN_PAD * 2 \
        + 4 * D_MODEL * TF * 2 + 2 * FFN_PAD * TD * 2 + 6 * ext * TF * 4 \
        + 2 * TM * D_MODEL * 4
    up_col = lambda i, j: (0, jnp.minimum(j, NF - 1))
    return pl.pallas_call(
        _ffn_kernel,
        grid=(t // TM, NF + nd),
        in_specs=[
            pl.BlockSpec((TM, D_MODEL), lambda i, j: (i, 0)),
            pl.BlockSpec((HALO, D_MODEL), lambda i, j: (jnp.maximum(i * (TM // HALO) - 1, 0), 0)),
            pl.BlockSpec((HALO, D_MODEL),
                         lambda i, j: (jnp.minimum((i + 1) * (TM // HALO), nh - 1), 0)),
            pl.BlockSpec((1, D_MODEL), lambda i, j: (0, 0)),
            pl.BlockSpec((1, D_MODEL), lambda i, j: (0, 0)),
            pl.BlockSpec((D_MODEL, TF), up_col),
            pl.BlockSpec((D_MODEL, TF), lambda i, j: (0, NF + jnp.minimum(j, NF - 1))),
            pl.BlockSpec((FFN_CONV_K, TF), up_col),
            pl.BlockSpec((1, TF), up_col),
            pl.BlockSpec((FFN_PAD, TD), lambda i, j: (0, jnp.clip(j - NF, 0, nd - 1))),
        ],
        out_specs=pl.BlockSpec((TM, D_MODEL), lambda i, j: (i, 0)),
        out_shape=jax.ShapeDtypeStruct((t, D_MODEL), F32),
        scratch_shapes=[pltpu.VMEM((ext, D_MODEL), BF16),
                        pltpu.VMEM((TM, FFN_PAD), BF16)],
        compiler_params=_params(est, 2),
        name="conv_ffn",
    )(h, h, h, gpre, gpost, w_up, w_up, w_dw, b_dw, w_down)


def _glu_kernel(h_ref, g_ref, wu_ref, wg_ref, bu_ref, bg_ref, o_ref, xn_ref, *, chunks_per_seq):
    i = pl.program_id(0)
    j = pl.program_id(1)

    @pl.when(j == 0)
    def _():
        xn_ref[...] = _rms(h_ref[...], g_ref[...]).astype(BF16)

    xn = xn_ref[...]
    u = jnp.dot(xn, wu_ref[...], preferred_element_type=F32) + bu_ref[...]
    gate = jnp.dot(xn, wg_ref[...], preferred_element_type=F32) + bg_ref[...]
    keep = _keep_rows(i, TM, chunks_per_seq)
    o_ref[...] = (u * jax.nn.sigmoid(gate) * keep).astype(BF16)


def _glu(h, g, w, b, chunks_per_seq):
    t = h.shape[0]
    nn = D_MODEL // TN_GLU
    est = 2 * TM * D_MODEL * 4 + TM * D_MODEL * 2 + 4 * D_MODEL * TN_GLU * 2 \
        + 2 * TM * TN_GLU * 2 + 4 * TM * TN_GLU * 4
    return pl.pallas_call(
        functools.partial(_glu_kernel, chunks_per_seq=chunks_per_seq),
        grid=(t // TM, nn),
        in_specs=[
            pl.BlockSpec((TM, D_MODEL), lambda i, j: (i, 0)),
            pl.BlockSpec((1, D_MODEL), lambda i, j: (0, 0)),
            pl.BlockSpec((D_MODEL, TN_GLU), lambda i, j: (0, j)),
            pl.BlockSpec((D_MODEL, TN_GLU), lambda i, j: (0, nn + j)),
            pl.BlockSpec((1, TN_GLU), lambda i, j: (0, j)),
            pl.BlockSpec((1, TN_GLU), lambda i, j: (0, nn + j)),
        ],
        out_specs=pl.BlockSpec((TM, TN_GLU), lambda i, j: (i, j)),
        out_shape=jax.ShapeDtypeStruct((t, D_MODEL), BF16),
        scratch_shapes=[pltpu.VMEM((TM, D_MODEL), BF16)],
        compiler_params=_params(est, 2),
        name="conf_glu",
    )(h, g, w, w, b, b)


def _dw_kernel(u_ref, up_ref, un_ref, wdw_ref, bdw_ref, lng_ref, lnb_ref, w2_ref, b2_ref,
               h_ref, gpost_ref, o_ref, ue_ref, cv_ref, y_ref, *, chunks_per_seq):
    i = pl.program_id(0)
    last = i == pl.num_programs(0) - 1
    for s in range(DW_SLABS):
        lanes = slice(s * V7X_LANES, (s + 1) * V7X_LANES)
        ue_ref[s, 0:HALO, :] = up_ref[:, lanes].astype(F32)
        ue_ref[s, HALO:HALO + TM_DW, :] = u_ref[:, lanes].astype(F32)
        ue_ref[s, HALO + TM_DW:, :] = jnp.where(last, 0.0, un_ref[:, lanes].astype(F32))

    shift = HALO - (CONV_K - 1) // 2

    def slab(s, carry):
        w = wdw_ref[s]
        bias = bdw_ref[s]
        for rb in range(TM_DW // DW_ROWS):
            r0 = rb * DW_ROWS
            acc = jnp.zeros((DW_ROWS, V7X_LANES), F32)
            for k in range(CONV_K):
                acc = acc + w[k:k + 1, :] * ue_ref[s, r0 + shift + k:r0 + shift + k + DW_ROWS, :]
            cv_ref[s, r0:r0 + DW_ROWS, :] = acc + bias
        return carry

    lax.fori_loop(0, DW_SLABS, slab, 0)

    for rb in range(TM_DW // DW_ROWS):
        rows = slice(rb * DW_ROWS, (rb + 1) * DW_ROWS)
        tot = cv_ref[0, rows, :]
        for s in range(1, DW_SLABS):
            tot = tot + cv_ref[s, rows, :]
        mu = jnp.sum(tot, axis=-1, keepdims=True) * (1.0 / D_MODEL)
        sq = jnp.square(cv_ref[0, rows, :] - mu)
        for s in range(1, DW_SLABS):
            sq = sq + jnp.square(cv_ref[s, rows, :] - mu)
        inv = lax.rsqrt(jnp.sum(sq, axis=-1, keepdims=True) * (1.0 / D_MODEL) + EPS)
        for s in range(DW_SLABS):
            lanes = slice(s * V7X_LANES, (s + 1) * V7X_LANES)
            ln = (cv_ref[s, rows, :] - mu) * inv * lng_ref[:, lanes] + lnb_ref[:, lanes]
            y_ref[rows, lanes] = (ln * jax.nn.sigmoid(ln)).astype(BF16)

    m = jnp.dot(y_ref[...], w2_ref[...], preferred_element_type=F32) + b2_ref[...]
    keep = _keep_rows(i, TM_DW, chunks_per_seq)
    o_ref[...] = h_ref[...] + _rms(m, gpost_ref[...]) * keep


def _dwconv(u, w_dw, b_dw, ln_g, ln_b, w2, b2, h, gpost, chunks_per_seq):
    t = u.shape[0]
    nh = t // HALO
    est = 2 * TM_DW * D_MODEL * 2 + (TM_DW + 2 * HALO) * D_MODEL * 4 + TM_DW * D_MODEL * 6 \
        + 2 * D_MODEL * D_MODEL * 2 + 4 * TM_DW * D_MODEL * 4 + 4 * TM_DW * D_MODEL * 4
    vec = pl.BlockSpec((1, D_MODEL), lambda i: (0, 0))
    return pl.pallas_call(
        functools.partial(_dw_kernel, chunks_per_seq=chunks_per_seq),
        grid=(t // TM_DW,),
        in_specs=[
            pl.BlockSpec((TM_DW, D_MODEL), lambda i: (i, 0)),
            pl.BlockSpec((HALO, D_MODEL), lambda i: (jnp.maximum(i * (TM_DW // HALO) - 1, 0), 0)),
            pl.BlockSpec((HALO, D_MODEL),
                         lambda i: (jnp.minimum((i + 1) * (TM_DW // HALO), nh - 1), 0)),
            pl.BlockSpec((DW_SLABS, CONV_K, V7X_LANES), lambda i: (0, 0, 0)),
            pl.BlockSpec((DW_SLABS, 1, V7X_LANES), lambda i: (0, 0, 0)),
            vec, vec,
            pl.BlockSpec((D_MODEL, D_MODEL), lambda i: (0, 0)),
            vec,
            pl.BlockSpec((TM_DW, D_MODEL), lambda i: (i, 0)),
            vec,
        ],
        out_specs=pl.BlockSpec((TM_DW, D_MODEL), lambda i: (i, 0)),
        out_shape=jax.ShapeDtypeStruct((t, D_MODEL), F32),
        scratch_shapes=[
            pltpu.VMEM((DW_SLABS, TM_DW + 2 * HALO, V7X_LANES), F32),
            pltpu.VMEM((DW_SLABS, TM_DW, V7X_LANES), F32),
            pltpu.VMEM((TM_DW, D_MODEL), BF16),
        ],
        compiler_params=_params(est, 1),
        name="conf_dw",
    )(u, u, u, w_dw, b_dw, ln_g, ln_b, w2, b2, h, gpost)


def _row(v):
    return v.reshape(1, -1).astype(F32)


def _encode(x, meta_tokens, p):
    batch, seq, _ = x.shape
    pad = (-(seq + N_META)) % CHUNK
    lp = seq + N_META + pad
    cps = lp // CHUNK
    meta = jnp.broadcast_to(meta_tokens[None].astype(x.dtype), (batch, N_META, D_MODEL))
    h = jnp.concatenate([jnp.zeros((batch, pad, D_MODEL), x.dtype), meta, x], axis=1)
    h = h.reshape(batch * lp, D_MODEL)

    half = RET_DK // 2
    inv = ROPE_BASE ** (-jnp.arange(half, dtype=F32) / half)
    ang = jnp.arange(lp).astype(F32)[:, None] * inv[None, :]
    cos = jnp.tile(jnp.cos(ang), (batch, 1))
    sin = jnp.tile(jnp.sin(ang), (batch, 1))

    for i in range(DEPTH):
        j = i // N_MIXERS
        if i % N_MIXERS == 0:
            proj = _proj(h, p["pre_mix"][i], p["ret_w_in"][j], cos, sin)
            gated = _ret_core(proj, p["ret_lg"][j], batch, lp)
            h = _outproj(gated, p["ret_w_out"][j], h, p["post_mix"][i])
        else:
            u = _glu(h, p["pre_mix"][i], p["conv_w_pw1"][j], p["conv_b_pw1"][j], cps)
            h = _dwconv(u, p["conv_w_dw"][j], p["conv_b_dw"][j], p["conv_ln_g"][j],
                        p["conv_ln_b"][j], p["conv_w_pw2"][j], p["conv_b_pw2"][j], h,
                        p["post_mix"][i], cps)
        h = _ffn(h, p["pre_ffn"][i], p["post_ffn"][i], p["ffn_w_up"][i], p["ffn_w_dw"][i],
                 p["ffn_b_dw"][i], p["ffn_w_down"][i])
    return h.reshape(batch, lp, D_MODEL)[:, pad + N_META:]


def kernel(x_prompt, x_sample, meta_tokens, norm_pre_mix, norm_post_mix, norm_pre_ffn, norm_post_ffn, ret_w_in, ret_decay_fwd, ret_decay_bwd, ret_w_out, conv_w_pw1, conv_b_pw1, conv_w_dw, conv_b_dw, conv_ln_g, conv_ln_b, conv_w_pw2, conv_b_pw2, ffn_w_up, ffn_w_dw, ffn_b_dw, ffn_w_down):
    fpad = FFN_PAD - FFN_HIDDEN
    w_up = jnp.concatenate(
        [jnp.pad(ffn_w_up[:, :, :FFN_HIDDEN], ((0, 0), (0, 0), (0, fpad))),
         jnp.pad(ffn_w_up[:, :, FFN_HIDDEN:], ((0, 0), (0, 0), (0, fpad)))],
        axis=-1).astype(BF16)
    p = {
        "pre_mix": [_row(v) for v in norm_pre_mix],
        "post_mix": [_row(v) for v in norm_post_mix],
        "pre_ffn": [_row(v) for v in norm_pre_ffn],
        "post_ffn": [_row(v) for v in norm_post_ffn],
        "ret_w_in": ret_w_in.astype(BF16),
        "ret_lg": jnp.stack([jax.nn.log_sigmoid(ret_decay_fwd.astype(F32)),
                             jax.nn.log_sigmoid(ret_decay_bwd.astype(F32))], axis=1),
        "ret_w_out": ret_w_out.astype(BF16),
        "conv_w_pw1": conv_w_pw1.astype(BF16),
        "conv_b_pw1": [_row(v) for v in conv_b_pw1],
        "conv_w_dw": conv_w_dw.astype(F32).reshape(-1, CONV_K, DW_SLABS, V7X_LANES)
                              .transpose(0, 2, 1, 3),
        "conv_b_dw": conv_b_dw.astype(F32).reshape(-1, DW_SLABS, 1, V7X_LANES),
        "conv_ln_g": [_row(v) for v in conv_ln_g],
        "conv_ln_b": [_row(v) for v in conv_ln_b],
        "conv_w_pw2": conv_w_pw2.astype(BF16),
        "conv_b_pw2": [_row(v) for v in conv_b_pw2],
        "ffn_w_up": w_up,
        "ffn_w_dw": jnp.pad(ffn_w_dw.astype(F32), ((0, 0), (0, 0), (0, fpad))),
        "ffn_b_dw": [_row(v) for v in jnp.pad(ffn_b_dw, ((0, 0), (0, fpad)))],
        "ffn_w_down": jnp.pad(ffn_w_down, ((0, 0), (0, fpad), (0, 0))).astype(BF16),
    }
    return (_encode(x_prompt, meta_tokens, p), _encode(x_sample, meta_tokens, p))
```
